```python
import jax, jax.numpy as jnp
from jax import lax
import numpy as np

D_MODEL = 1024
BATCH = 16
SEQ = 2048
DEPTH = 1
DEC_BATCH = 32
DEC_SEQ = 16
PAST_LEN = 1024

CHUNK = 64
N_HEADS = 8
HEAD_DIM = 64
ATTN_WIDTH = N_HEADS * HEAD_DIM
POOL_WINDOWS = (2, 4, 8, 16)
N_POOL_GROUPS = len(POOL_WINDOWS)
POOL_WIDTH = D_MODEL // 2
POOL_GROUP = POOL_WIDTH // N_POOL_GROUPS
POOL_STATE = max(POOL_WINDOWS) - 1
D_FF = 2816
CONV_WIDTH = 3
Q_BLOCK = 128
EPS = 1e-6
IN_WIDTH = 3 * ATTN_WIDTH + POOL_WIDTH + 2 * D_MODEL

kernel_name = "stick_breaking_pool_hybrid_stream_step"


def rmsnorm(x, g):
    x32 = x.astype(jnp.float32)
    y = x32 * lax.rsqrt(jnp.mean(x32 * x32, axis=-1, keepdims=True) + EPS) * g.astype(jnp.float32)
    return y.astype(x.dtype)


def stick_breaking(q, k, v, q_pos, k_pos):
    z = jnp.einsum('bqhd,bkhd->bhqk', q.astype(jnp.float32), k.astype(jnp.float32)) * (HEAD_DIM ** -0.5)
    mask = (k_pos[None, :] < q_pos[:, None])[None, None]
    log_fail = jnp.where(mask, jax.nn.log_sigmoid(-z), 0.0)
    log_between = lax.cumsum(log_fail, axis=3, reverse=True) - log_fail
    a = jnp.where(mask, jnp.exp(jax.nn.log_sigmoid(z) + log_between), 0.0)
    return jnp.einsum('bhqk,bkhd->bqhd', a, v.astype(jnp.float32)).astype(v.dtype)


def stick_breaking_prompt(q, k, v):
    T = q.shape[1]
    pos = jnp.arange(T)
    outs = []
    for i in range(T // Q_BLOCK):
        lo, hi = i * Q_BLOCK, (i + 1) * Q_BLOCK
        outs.append(stick_breaking(q[:, lo:hi], k[:, :hi], v[:, :hi], pos[lo:hi], pos[:hi]))
    return jnp.concatenate(outs, axis=1)


def multi_scale_pool(u, left, pos0):
    T = u.shape[1]
    ext_raw = jnp.concatenate([left, u], axis=1)
    cs = jnp.cumsum(ext_raw.astype(jnp.float32), axis=1)
    cs = jnp.concatenate([jnp.zeros_like(cs[:, :1]), cs], axis=1)
    pos = pos0 + jnp.arange(T)
    end = POOL_STATE + 1
    groups = []
    for g, w in enumerate(POOL_WINDOWS):
        sl = slice(g * POOL_GROUP, (g + 1) * POOL_GROUP)
        win = cs[:, end:end + T, sl] - cs[:, end - w:end - w + T, sl]
        cnt = jnp.minimum(pos + 1, w).astype(jnp.float32)[None, :, None]
        groups.append(win / cnt)
    mean = jnp.concatenate(groups, axis=-1)
    return mean - u.astype(jnp.float32), ext_raw[:, -POOL_STATE:]


def causal_dwconv(h, left, w, b):
    T = h.shape[1]
    ext = jnp.concatenate([left, h], axis=1)
    out = b.astype(jnp.float32)
    for j in range(CONV_WIDTH):
        out = out + ext[:, j:j + T].astype(jnp.float32) * w[j].astype(jnp.float32)
    return out.astype(h.dtype), ext[:, -(CONV_WIDTH - 1):]


def trunk_layer(x, pos0, kv_past, pool_left, conv_left, norm_mix, w_in, w_a, w_b,
                pool_w, pool_scale, w_o, norm_ffn, w_up, conv_w, conv_b, w_down):
    B, T, _ = x.shape
    h = rmsnorm(x, norm_mix)
    proj = h @ w_in
    cuts = [ATTN_WIDTH, 2 * ATTN_WIDTH, 3 * ATTN_WIDTH,
            3 * ATTN_WIDTH + POOL_WIDTH, 3 * ATTN_WIDTH + POOL_WIDTH + D_MODEL]
    q, k, v, u, g_a, g_b = jnp.split(proj, cuts, axis=-1)
    q = q.reshape(B, T, N_HEADS, HEAD_DIM)
    k = k.reshape(B, T, N_HEADS, HEAD_DIM)
    v = v.reshape(B, T, N_HEADS, HEAD_DIM)

    if kv_past is None:
        attn = stick_breaking_prompt(q, k, v)
    else:
        ck, cv = kv_past
        past = ck.shape[1]
        k_all = jnp.concatenate([ck, k], axis=1)
        v_all = jnp.concatenate([cv, v], axis=1)
        attn = stick_breaking(q, k_all, v_all, past + jnp.arange(T), jnp.arange(past + T))
    branch_a = attn.reshape(B, T, ATTN_WIDTH) @ w_a

    pooled, pool_new = multi_scale_pool(u, pool_left, pos0)
    pmix = jnp.einsum('btgc,gcd->btgd', pooled.reshape(B, T, N_POOL_GROUPS, POOL_GROUP),
                      pool_w.astype(jnp.float32)).reshape(B, T, POOL_WIDTH)
    pmix = pmix * pool_scale.astype(jnp.float32)
    branch_b = pmix.astype(x.dtype) @ w_b

    merged = jax.nn.sigmoid(g_a) * branch_a + jax.nn.sigmoid(g_b) * branch_b
    x = x + merged @ w_o

    h2 = rmsnorm(x, norm_ffn)
    up = h2 @ w_up
    c, conv_new = causal_dwconv(up, conv_left, conv_w, conv_b)
    gate, val = jnp.split(c, [D_FF], axis=-1)
    x = x + (jax.nn.gelu(gate) * val) @ w_down
    return x, k, v, pool_new, conv_new


def setup_inputs(seed: int = 0) -> dict:
    key = jax.random.key(seed)
    ks = jax.random.split(key, 24)
    f32 = jnp.float32
    nrm = lambda k, shape, scale: jax.random.normal(k, shape, f32) * scale
    return {
        "x_prompt": nrm(ks[0], (BATCH, SEQ, D_MODEL), 1.0),
        "x_sample": nrm(ks[1], (DEC_BATCH, DEC_SEQ, D_MODEL), 1.0),
        "cache_k": nrm(ks[2], (DEPTH, DEC_BATCH, PAST_LEN, N_HEADS, HEAD_DIM), 1.0),
        "cache_v": nrm(ks[3], (DEPTH, DEC_BATCH, PAST_LEN, N_HEADS, HEAD_DIM), 1.0),
        "state_pool": nrm(ks[4], (DEPTH, DEC_BATCH, POOL_STATE, POOL_WIDTH), 1.0),
        "state_conv": nrm(ks[5], (DEPTH, DEC_BATCH, CONV_WIDTH - 1, 2 * D_FF), 1.0),
        "norm_mix": 1.0 + nrm(ks[6], (DEPTH, D_MODEL), 0.02),
        "w_in": nrm(ks[7], (DEPTH, D_MODEL, IN_WIDTH), D_MODEL ** -0.5),
        "w_a": nrm(ks[8], (DEPTH, ATTN_WIDTH, D_MODEL), ATTN_WIDTH ** -0.5),
        "w_b": nrm(ks[9], (DEPTH, POOL_WIDTH, D_MODEL), POOL_WIDTH ** -0.5),
        "pool_w": nrm(ks[10], (DEPTH, N_POOL_GROUPS, POOL_GROUP, POOL_GROUP), POOL_GROUP ** -0.5),
        "pool_scale": 1.0 + nrm(ks[11], (DEPTH, POOL_WIDTH), 0.02),
        "w_o": nrm(ks[12], (DEPTH, D_MODEL, D_MODEL), D_MODEL ** -0.5),
        "norm_ffn": 1.0 + nrm(ks[13], (DEPTH, D_MODEL), 0.02),
        "w_up": nrm(ks[14], (DEPTH, D_MODEL, 2 * D_FF), D_MODEL ** -0.5),
        "conv_w": nrm(ks[15], (DEPTH, CONV_WIDTH, 2 * D_FF), CONV_WIDTH ** -0.5),
        "conv_b": nrm(ks[16], (DEPTH, 2 * D_FF), 0.01),
        "w_down": nrm(ks[17], (DEPTH, D_FF, D_MODEL), D_FF ** -0.5),
        "norm_final": 1.0 + nrm(ks[18], (D_MODEL,), 0.02),
    }


def reference(x_prompt, x_sample, cache_k, cache_v, state_pool, state_conv,
              norm_mix, w_in, w_a, w_b, pool_w, pool_scale, w_o, norm_ffn,
              w_up, conv_w, conv_b, w_down, norm_final):
    xp, xs = x_prompt, x_sample
    kp, vp, pp, cp = [], [], [], []
    ksl, vsl, psl, csl = [], [], [], []
    for l in range(DEPTH):
        weights = (norm_mix[l], w_in[l], w_a[l], w_b[l], pool_w[l], pool_scale[l], w_o[l],
                   norm_ffn[l], w_up[l], conv_w[l], conv_b[l], w_down[l])
        pool_zero = jnp.zeros((xp.shape[0], POOL_STATE, POOL_WIDTH), xp.dtype)
        conv_zero = jnp.zeros((xp.shape[0], CONV_WIDTH - 1, 2 * D_FF), xp.dtype)
        xp, k_new, v_new, pool_new, conv_new = trunk_layer(
            xp, 0, None, pool_zero, conv_zero, *weights)
        kp.append(k_new); vp.append(v_new); pp.append(pool_new); cp.append(conv_new)
        xs, k_new, v_new, pool_new, conv_new = trunk_layer(
            xs, cache_k.shape[2], (cache_k[l], cache_v[l]), state_pool[l], state_conv[l], *weights)
        ksl.append(k_new); vsl.append(v_new); psl.append(pool_new); csl.append(conv_new)
    y_prompt = rmsnorm(xp, norm_final)
    y_sample = rmsnorm(xs, norm_final)
    return (y_prompt, y_sample,
            jnp.stack(kp), jnp.stack(vp), jnp.stack(pp), jnp.stack(cp),
            jnp.stack(ksl), jnp.stack(vsl), jnp.stack(psl), jnp.stack(csl))
```

```python
import functools

import jax
import jax.numpy as jnp
from jax import lax
from jax.experimental import pallas as pl
from jax.experimental.pallas import tpu as pltpu

D_MODEL = 1024
N_HEADS = 8
HEAD_DIM = 64
ATTN_WIDTH = N_HEADS * HEAD_DIM
POOL_WINDOWS = (2, 4, 8, 16)
POOL_WIDTH = 512
POOL_GROUP = POOL_WIDTH // len(POOL_WINDOWS)
POOL_STATE = max(POOL_WINDOWS) - 1
D_FF = 2816
CONV_WIDTH = 3
EPS = 1e-6
IN_WIDTH = 3 * ATTN_WIDTH + POOL_WIDTH + 2 * D_MODEL

LANES = 128
HEADS_PER_LANE_TILE = LANES // HEAD_DIM
POOL_HALO = 16
CONV_HALO = 8
FF_CHUNK = 256
VMEM_LIMIT = 56 * 1024 * 1024

F32 = jnp.float32
BF16 = jnp.bfloat16


def _rmsnorm(x, g):
    return x * lax.rsqrt(jnp.mean(x * x, axis=-1, keepdims=True) + EPS) * g


def _const_spec(shape):
    nd = len(shape)
    return pl.BlockSpec(shape, lambda *_: (0,) * nd, pipeline_mode=pl.Buffered(1))


def _inproj_kernel(x_ref, g_ref, w_ref, q_ref, k_ref, v_ref, u_ref, kb_ref, vb_ref, ga_ref, gb_ref):
    h = _rmsnorm(x_ref[...], g_ref[...]).astype(BF16)

    def proj(lo, hi):
        return jnp.dot(h, w_ref[:, lo:hi], preferred_element_type=F32)

    a = ATTN_WIDTH
    q_ref[...] = (proj(0, a) * (HEAD_DIM ** -0.5)).astype(BF16)
    k = proj(a, 2 * a)
    k_ref[...] = k
    kb_ref[...] = k.astype(BF16)
    v = proj(2 * a, 3 * a)
    v_ref[...] = v
    vb_ref[...] = v.astype(BF16)
    u0 = 3 * a
    u_ref[...] = proj(u0, u0 + POOL_WIDTH)
    g0 = u0 + POOL_WIDTH
    ga_ref[...] = jax.nn.sigmoid(proj(g0, g0 + D_MODEL)).astype(BF16)
    gb_ref[...] = jax.nn.sigmoid(proj(g0 + D_MODEL, g0 + 2 * D_MODEL)).astype(BF16)


def _inproj(x2d, norm_mix, w_in_bf, tm):
    m = x2d.shape[0]
    row = lambda w: pl.BlockSpec((tm, w), lambda i: (i, 0))
    outs = [(ATTN_WIDTH, BF16), (ATTN_WIDTH, F32), (ATTN_WIDTH, F32), (POOL_WIDTH, F32),
            (ATTN_WIDTH, BF16), (ATTN_WIDTH, BF16), (D_MODEL, BF16), (D_MODEL, BF16)]
    return pl.pallas_call(
        _inproj_kernel,
        grid=(m // tm,),
        in_specs=[row(D_MODEL), _const_spec((1, D_MODEL)), _const_spec((D_MODEL, IN_WIDTH))],
        out_specs=[row(w) for w, _ in outs],
        out_shape=[jax.ShapeDtypeStruct((m, w), dt) for w, dt in outs],
        compiler_params=pltpu.CompilerParams(dimension_semantics=("arbitrary",),
                                             vmem_limit_bytes=VMEM_LIMIT),
        name="inproj",
    )(x2d, norm_mix.reshape(1, D_MODEL), w_in_bf)


def _softplus(z):
    return jnp.maximum(z, 0.0) + jnp.log(1.0 + jnp.exp(-jnp.abs(z)))


def _stick_block(qrows, kblk, vblk, tri, carry, mask):
    z = lax.dot_general(qrows, kblk, (((1,), (1,)), ((), ())), preferred_element_type=F32)
    sp = _softplus(z)
    if mask is not None:
        sp = jnp.where(mask, sp, 0.0)
    hi = sp.astype(BF16)
    lo = (sp - hi.astype(F32)).astype(BF16)
    later = jnp.dot(jnp.concatenate([hi, lo], axis=1), tri, preferred_element_type=F32)
    a = jnp.exp(z - sp - later - carry)
    if mask is not None:
        a = jnp.where(mask, a, 0.0)
    pv = jnp.dot(a.astype(BF16), vblk, preferred_element_type=F32)
    return pv, carry + jnp.sum(sp, axis=1, keepdims=True)


def _attn_prompt_kernel(q_ref, k_ref, v_ref, tri_ref, o_ref, *, blk):
    qi = pl.program_id(2)
    q2 = q_ref[0]
    lane = lax.broadcasted_iota(jnp.int32, q2.shape, 1)
    zero = jnp.zeros_like(q2)
    qrows = jnp.concatenate([jnp.where(lane < HEAD_DIM, q2, zero),
                             jnp.where(lane >= HEAD_DIM, q2, zero)], axis=0)
    rows = 2 * blk
    tri = tri_ref[...]

    r = lax.broadcasted_iota(jnp.int32, (rows, blk), 0)
    c = lax.broadcasted_iota(jnp.int32, (rows, blk), 1)
    diag_mask = c < (r & (blk - 1))

    def kv(j):
        start = pl.multiple_of(j * blk, blk)
        return k_ref[0, pl.ds(start, blk), :], v_ref[0, pl.ds(start, blk), :]

    kblk, vblk = kv(qi)
    acc, carry = _stick_block(qrows, kblk, vblk, tri, jnp.zeros((rows, 1), F32), diag_mask)

    def body(i, state):
        acc, carry = state
        kblk, vblk = kv(qi - 1 - i)
        pv, carry = _stick_block(qrows, kblk, vblk, tri, carry, None)
        return acc + pv, carry

    acc, _ = lax.fori_loop(0, qi, body, (acc, carry))
    lane_o = lax.broadcasted_iota(jnp.int32, (blk, LANES), 1)
    o_ref[0] = jnp.where(lane_o < HEAD_DIM, acc[:blk], acc[blk:]).astype(o_ref.dtype)


def _tri(blk):
    j = lax.broadcasted_iota(jnp.int32, (blk, blk), 0)
    s = lax.broadcasted_iota(jnp.int32, (blk, blk), 1)
    t = (j > s).astype(BF16)
    return jnp.concatenate([t, t], axis=0)


def _attn_prompt(q, kb, vb, blk):
    b, t, _ = q.shape
    n_tiles = ATTN_WIDTH // LANES
    qspec = pl.BlockSpec((1, blk, LANES), lambda bi, hi, qi: (bi, qi, hi))
    kvspec = pl.BlockSpec((1, t, LANES), lambda bi, hi, qi: (bi, 0, hi))
    return pl.pallas_call(
        functools.partial(_attn_prompt_kernel, blk=blk),
        grid=(b, n_tiles, t // blk),
        in_specs=[qspec, kvspec, kvspec, _const_spec((2 * blk, blk))],
        out_specs=qspec,
        out_shape=jax.ShapeDtypeStruct((b, t, ATTN_WIDTH), BF16),
        compiler_params=pltpu.CompilerParams(
            dimension_semantics=("arbitrary", "arbitrary", "arbitrary"),
            vmem_limit_bytes=VMEM_LIMIT),
        name="attn_prompt",
    )(q, kb, vb, _tri(blk))


def _attn_sample_kernel(q_ref, kn_ref, vn_ref, ck_ref, cv_ref, tri_ref, o_ref, *, blk, new_blk):
    tq = q_ref.shape[1]
    rows = N_HEADS * tq
    q = q_ref[0]
    qrep = jnp.concatenate([q] * N_HEADS, axis=0)
    r = lax.broadcasted_iota(jnp.int32, (rows, ATTN_WIDTH), 0)
    c = lax.broadcasted_iota(jnp.int32, (rows, ATTN_WIDTH), 1)
    tq_bits = tq.bit_length() - 1
    head_bits = HEAD_DIM.bit_length() - 1
    assert tq == 1 << tq_bits
    head_sel = (r >> tq_bits) == (c >> head_bits)
    qrows = jnp.where(head_sel, qrep, jnp.zeros_like(qrep))
    tri = tri_ref[...]

    pad = jnp.zeros((new_blk - tq, ATTN_WIDTH), BF16)
    kblk = jnp.concatenate([kn_ref[0], pad], axis=0)
    vblk = jnp.concatenate([vn_ref[0], pad], axis=0)
    rn = lax.broadcasted_iota(jnp.int32, (rows, new_blk), 0)
    cn = lax.broadcasted_iota(jnp.int32, (rows, new_blk), 1)
    new_mask = cn < (rn & (tq - 1))
    tri_new = jnp.concatenate([tri[:new_blk, :new_blk], tri[:new_blk, :new_blk]], axis=0)
    acc, carry = _stick_block(qrows, kblk, vblk, tri_new, jnp.zeros((rows, 1), F32), new_mask)

    past = ck_ref.shape[1]
    for j in reversed(range(past // blk)):
        kblk = ck_ref[0, j * blk:(j + 1) * blk, :].astype(BF16)
        vblk = cv_ref[0, j * blk:(j + 1) * blk, :].astype(BF16)
        pv, carry = _stick_block(qrows, kblk, vblk, tri, carry, None)
        acc = acc + pv

    co = lax.broadcasted_iota(jnp.int32, (tq, ATTN_WIDTH), 1) >> head_bits
    out = jnp.zeros((tq, ATTN_WIDTH), F32)
    for h in range(N_HEADS):
        out = jnp.where(co == h, acc[h * tq:(h + 1) * tq], out)
    o_ref[0] = out.astype(o_ref.dtype)


def _attn_sample(q, kb, vb, cache_k, cache_v, blk, new_blk):
    b, tq, _ = q.shape
    past = cache_k.shape[1]
    new = pl.BlockSpec((1, tq, ATTN_WIDTH), lambda bi: (bi, 0, 0))
    old = pl.BlockSpec((1, past, ATTN_WIDTH), lambda bi: (bi, 0, 0))
    return pl.pallas_call(
        functools.partial(_attn_sample_kernel, blk=blk, new_blk=new_blk),
        grid=(b,),
        in_specs=[new, new, new, old, old, _const_spec((2 * blk, blk))],
        out_specs=new,
        out_shape=jax.ShapeDtypeStruct((b, tq, ATTN_WIDTH), BF16),
        compiler_params=pltpu.CompilerParams(dimension_semantics=("arbitrary",),
                                             vmem_limit_bytes=VMEM_LIMIT),
        name="attn_sample",
    )(q, kb, vb, cache_k, cache_v, _tri(blk))


def _with_history(hist, cur):
    nb, h, c = hist.shape
    t = cur.shape[1]
    return jnp.concatenate([hist, cur], axis=1).reshape(nb * (h + t), c)


def _drop_history(flat, nb, h, t):
    c = flat.shape[-1]
    return flat.reshape(nb, h + t, c)[:, h:, :].reshape(nb * t, c)


def _gelu_tanh(x):
    return 0.5 * x * (1.0 + jnp.tanh(0.7978845608028654 * (x + 0.044715 * (x * x * x))))


def _post_kernel(x_ref, attn_ref, u_ref, ga_ref, gb_ref, pleft_ref, cleft_ref,
                 wa_ref, wb_ref, pw_ref, ps_ref, wo_ref, nffn_ref, wup_ref, cw_ref, cb_ref,
                 wdn_ref, nfin_ref,
                 y_ref, ptail_ref, ctail_ref,
                 uhist_ref, chist_ref, *, pos0):
    nb, tt, _ = x_ref.shape
    m = nb * tt
    ti = pl.program_id(1)

    @pl.when(ti == 0)
    def _():
        uhist_ref[...] = pleft_ref[...]
        chist_ref[...] = cleft_ref[...]

    x = x_ref[...].reshape(m, D_MODEL)
    u3 = u_ref[...]
    u = u3.reshape(m, POOL_WIDTH)

    ext = _with_history(uhist_ref[...], u3)
    row = lax.broadcasted_iota(jnp.int32, (m, POOL_GROUP), 0)
    pos = pos0 + ti * tt + (row & (tt - 1))
    pmix = []
    for g, w in enumerate(POOL_WINDOWS):
        sl = slice(g * POOL_GROUP, (g + 1) * POOL_GROUP)
        s = ext[:, sl]
        shift = 1
        while shift < w:
            s = s + pltpu.roll(s, shift, axis=0)
            shift *= 2
        win = _drop_history(s, nb, POOL_HALO, tt)
        cnt = jnp.minimum(pos + 1, w).astype(F32)
        pooled = win / cnt - u[:, sl]
        pmix.append(jnp.dot(pooled.astype(BF16), pw_ref[g], preferred_element_type=F32))
    pmix = jnp.concatenate(pmix, axis=1) * ps_ref[...]
    branch_b = jnp.dot(pmix.astype(BF16), wb_ref[...], preferred_element_type=F32)
    branch_a = jnp.dot(attn_ref[...].reshape(m, ATTN_WIDTH), wa_ref[...], preferred_element_type=F32)
    merged = (ga_ref[...].reshape(m, D_MODEL).astype(F32) * branch_a
              + gb_ref[...].reshape(m, D_MODEL).astype(F32) * branch_b)
    x1 = x + jnp.dot(merged.astype(BF16), wo_ref[...], preferred_element_type=F32)

    tail = u3[:, tt - POOL_HALO:, :]
    uhist_ref[...] = tail
    ptail_ref[...] = tail

    h2 = _rmsnorm(x1, nffn_ref[...]).astype(BF16)
    acc = jnp.zeros((m, D_MODEL), F32)
    for j in range(D_FF // FF_CHUNK):
        halves = []
        for half in range(2):
            c0 = half * D_FF + j * FF_CHUNK
            cs = slice(c0, c0 + FF_CHUNK)
            up = jnp.dot(h2, wup_ref[:, cs], preferred_element_type=F32)
            up3 = up.reshape(nb, tt, FF_CHUNK)
            e = _with_history(chist_ref[:, :, cs], up3)
            conv = cb_ref[:, cs] + up * cw_ref[2:3, cs]
            for d in (1, 2):
                shifted = _drop_history(pltpu.roll(e, d, axis=0), nb, CONV_HALO, tt)
                conv = conv + shifted * cw_ref[2 - d:3 - d, cs]
            halves.append(conv)
            ctile = up3[:, tt - CONV_HALO:, :]
            chist_ref[:, :, cs] = ctile
            ctail_ref[:, :, cs] = ctile
        act = (_gelu_tanh(halves[0]) * halves[1]).astype(BF16)
        acc = acc + jnp.dot(act, wdn_ref[j * FF_CHUNK:(j + 1) * FF_CHUNK, :],
                            preferred_element_type=F32)
    y = _rmsnorm(x1 + acc, nfin_ref[...])
    y_ref[...] = y.reshape(nb, tt, D_MODEL)


def _post(x, attn, u, ga, gb, pool_left, conv_left, weights, nb, tt, pos0):
    b, t, _ = x.shape
    (w_a, w_b, pool_w, pool_scale, w_o, norm_ffn, w_up, conv_w, conv_b, w_down, norm_final) = weights
    tile = lambda w: pl.BlockSpec((nb, tt, w), lambda bi, ti: (bi, ti, 0))
    per_seq = lambda r, w: pl.BlockSpec((nb, r, w), lambda bi, ti: (bi, 0, 0))
    consts = [w_a, w_b, pool_w, pool_scale, w_o, norm_ffn, w_up, conv_w, conv_b, w_down, norm_final]
    return pl.pallas_call(
        functools.partial(_post_kernel, pos0=pos0),
        grid=(b // nb, t // tt),
        in_specs=[tile(D_MODEL), tile(ATTN_WIDTH), tile(POOL_WIDTH), tile(D_MODEL), tile(D_MODEL),
                  per_seq(POOL_HALO, POOL_WIDTH), per_seq(CONV_HALO, 2 * D_FF)]
                 + [_const_spec(c.shape) for c in consts],
        out_specs=[tile(D_MODEL), per_seq(POOL_HALO, POOL_WIDTH), per_seq(CONV_HALO, 2 * D_FF)],
        out_shape=[jax.ShapeDtypeStruct((b, t, D_MODEL), F32),
                   jax.ShapeDtypeStruct((b, POOL_HALO, POOL_WIDTH), F32),
                   jax.ShapeDtypeStruct((b, CONV_HALO, 2 * D_FF), F32)],
        scratch_shapes=[pltpu.VMEM((nb, POOL_HALO, POOL_WIDTH), F32),
                        pltpu.VMEM((nb, CONV_HALO, 2 * D_FF), F32)],
        compiler_params=pltpu.CompilerParams(dimension_semantics=("arbitrary", "arbitrary"),
                                             vmem_limit_bytes=VMEM_LIMIT),
        name="post",
    )(x, attn, u, ga, gb, pool_left, conv_left, *consts)


def _left_pad(state, rows):
    b, r, c = state.shape
    return jnp.concatenate([jnp.zeros((b, rows - r, c), state.dtype), state], axis=1)


def _stream(x, past, pool_left, conv_left, norm_mix, w_in_bf, post_weights, *, tm, nb, tt, pos0):
    b, t, _ = x.shape
    m = b * t
    q, k, v, u, kb, vb, ga, gb = _inproj(x.reshape(m, D_MODEL), norm_mix, w_in_bf, tm)
    seq = lambda a: a.reshape(b, t, a.shape[-1])
    if past is None:
        attn = _attn_prompt(seq(q), seq(kb), seq(vb), blk=256)
    else:
        ck, cv = past
        attn = _attn_sample(seq(q), seq(kb), seq(vb),
                            ck.reshape(b, -1, ATTN_WIDTH), cv.reshape(b, -1, ATTN_WIDTH),
                            blk=256, new_blk=128)
    y, ptail, ctail = _post(x, attn, seq(u), seq(ga), seq(gb),
                            _left_pad(pool_left, POOL_HALO), _left_pad(conv_left, CONV_HALO),
                            post_weights, nb, tt, pos0)
    heads = lambda a: a.reshape(1, b, t, N_HEADS, HEAD_DIM)
    return (y, heads(k), heads(v),
            ptail[None, :, POOL_HALO - POOL_STATE:, :],
            ctail[None, :, CONV_HALO - (CONV_WIDTH - 1):, :])


def kernel(x_prompt, x_sample, cache_k, cache_v, state_pool, state_conv, norm_mix, w_in, w_a, w_b,
           pool_w, pool_scale, w_o, norm_ffn, w_up, conv_w, conv_b, w_down, norm_final):
    assert w_in.shape[0] == 1, "single-layer trunk"
    row = lambda a: a.reshape(1, -1)
    post_weights = (w_a[0].astype(BF16), w_b[0].astype(BF16), pool_w[0].astype(BF16),
                    row(pool_scale[0]), w_o[0].astype(BF16), row(norm_ffn[0]),
                    w_up[0].astype(BF16), conv_w[0], row(conv_b[0]), w_down[0].astype(BF16),
                    row(norm_final))
    w_in_bf = w_in[0].astype(BF16)
    bp = x_prompt.shape[0]
    zeros_pool = jnp.zeros((bp, POOL_STATE, POOL_WIDTH), F32)
    zeros_conv = jnp.zeros((bp, CONV_WIDTH - 1, 2 * D_FF), F32)
    yp, kp, vp, pp, cp = _stream(x_prompt, None, zeros_pool, zeros_conv, norm_mix[0], w_in_bf,
                                 post_weights, tm=512, nb=1, tt=512, pos0=0)
    bs, ts, _ = x_sample.shape
    ys, ks, vs, ps, cs = _stream(x_sample, (cache_k[0], cache_v[0]), state_pool[0], state_conv[0],
                                 norm_mix[0], w_in_bf, post_weights,
                                 tm=bs * ts, nb=bs, tt=ts, pos0=cache_k.shape[2])
    return (yp, ys, kp, vp, pp, cp, ks, vs, ps, cs)
```

```python
import functools

import jax
import jax.numpy as jnp
from jax import lax
from jax.experimental import pallas as pl
from jax.experimental.pallas import tpu as pltpu

D_MODEL = 1024
N_HEADS = 8
HEAD_DIM = 64
ATTN_WIDTH = N_HEADS * HEAD_DIM
POOL_WINDOWS = (2, 4, 8, 16)
POOL_WIDTH = 512
POOL_GROUP = POOL_WIDTH // len(POOL_WINDOWS)
POOL_STATE = max(POOL_WINDOWS) - 1
D_FF = 2816
CONV_WIDTH = 3
EPS = 1e-6
IN_WIDTH = 3 * ATTN_WIDTH + POOL_WIDTH + 2 * D_MODEL

LANES = 128
HEADS_PER_LANE_TILE = LANES // HEAD_DIM
POOL_HALO = 16
CONV_HALO = 8
FF_CHUNK = 256
VMEM_LIMIT = 56 * 1024 * 1024
LOG2E = 1.4426950408889634
DEAD_LOG2 = 160.0

F32 = jnp.float32
BF16 = jnp.bfloat16


def _rmsnorm(x, g):
    return x * lax.rsqrt(jnp.mean(x * x, axis=-1, keepdims=True) + EPS) * g


def _const_spec(shape):
    nd = len(shape)
    return pl.BlockSpec(shape, lambda *_: (0,) * nd, pipeline_mode=pl.Buffered(1))


def _inproj_kernel(x_ref, g_ref, w_ref, q_ref, k_ref, v_ref, u_ref, kb_ref, vb_ref, ga_ref, gb_ref,
                   *, kv_transposed):
    h = _rmsnorm(x_ref[...], g_ref[...]).astype(BF16)

    def proj(lo, hi):
        return jnp.dot(h, w_ref[:, lo:hi], preferred_element_type=F32)

    a = ATTN_WIDTH
    q_ref[...] = (proj(0, a) * (HEAD_DIM ** -0.5 * LOG2E)).astype(BF16)
    k = proj(a, 2 * a)
    kb_ref[...] = k.astype(BF16)
    v = proj(2 * a, 3 * a)
    vb_ref[...] = v.astype(BF16)
    if kv_transposed:
        k_ref[0] = k.T
        v_ref[0] = v.T
    else:
        k_ref[...] = k
        v_ref[...] = v
    u0 = 3 * a
    u_ref[...] = proj(u0, u0 + POOL_WIDTH)
    g0 = u0 + POOL_WIDTH
    ga_ref[...] = jax.nn.sigmoid(proj(g0, g0 + D_MODEL)).astype(BF16)
    gb_ref[...] = jax.nn.sigmoid(proj(g0 + D_MODEL, g0 + 2 * D_MODEL)).astype(BF16)


def _inproj(x2d, norm_mix, w_in_bf, tm, seq_len, kv_transposed):
    m = x2d.shape[0]
    row = lambda w: pl.BlockSpec((tm, w), lambda i: (i, 0))
    outs = [(ATTN_WIDTH, BF16), (ATTN_WIDTH, F32), (ATTN_WIDTH, F32), (POOL_WIDTH, F32),
            (ATTN_WIDTH, BF16), (ATTN_WIDTH, BF16), (D_MODEL, BF16), (D_MODEL, BF16)]
    out_specs = [row(w) for w, _ in outs]
    out_shape = [jax.ShapeDtypeStruct((m, w), dt) for w, dt in outs]
    if kv_transposed:
        tiles = seq_len // tm
        for i in (1, 2):
            out_specs[i] = pl.BlockSpec((1, ATTN_WIDTH, tm), lambda i: (i // tiles, 0, i % tiles))
            out_shape[i] = jax.ShapeDtypeStruct((m // seq_len, ATTN_WIDTH, seq_len), F32)
    return pl.pallas_call(
        functools.partial(_inproj_kernel, kv_transposed=kv_transposed),
        grid=(m // tm,),
        in_specs=[row(D_MODEL), _const_spec((1, D_MODEL)), _const_spec((D_MODEL, IN_WIDTH))],
        out_specs=out_specs,
        out_shape=out_shape,
        compiler_params=pltpu.CompilerParams(dimension_semantics=("arbitrary",),
                                             vmem_limit_bytes=VMEM_LIMIT),
        name="inproj",
    )(x2d, norm_mix.reshape(1, D_MODEL), w_in_bf)


_NT = (((1,), (1,)), ((), ()))


def _softplus2(z):
    return jnp.maximum(z, 0.0) + jnp.log2(1.0 + jnp.exp2(-jnp.abs(z)))


def _stick_scores(qrows, kblk, tri, mask, k_transposed):
    if k_transposed:
        z = jnp.dot(qrows, kblk, preferred_element_type=F32)
    else:
        z = lax.dot_general(qrows, kblk, _NT, preferred_element_type=F32)
    sp = _softplus2(z)
    if mask is not None:
        sp = jnp.where(mask, sp, 0.0)
    hi = sp.astype(BF16)
    lo = (sp - hi.astype(F32)).astype(BF16)
    later = jnp.dot(jnp.concatenate([hi, lo], axis=1), tri, preferred_element_type=F32)
    return z, sp, later


def _stick_values(z, sp, later, carry, vblk, mask, v_transposed):
    a = jnp.exp2(z - sp - later - carry)
    if mask is not None:
        a = jnp.where(mask, a, 0.0)
    a = a.astype(BF16)
    if v_transposed:
        return lax.dot_general(a, vblk, _NT, preferred_element_type=F32)
    return jnp.dot(a, vblk, preferred_element_type=F32)


def _row_sum(sp):
    return jnp.sum(sp, axis=1, keepdims=True)


def _attn_prompt_kernel(q_ref, k_ref, v_ref, tri_ref, o_ref, *, blk):
    qi = pl.program_id(2)
    q2 = q_ref[0]
    lane = lax.broadcasted_iota(jnp.int32, q2.shape, 1)
    zero = jnp.zeros_like(q2)
    qrows = jnp.concatenate([jnp.where(lane < HEAD_DIM, q2, zero),
                             jnp.where(lane >= HEAD_DIM, q2, zero)], axis=0)
    rows = 2 * blk
    tri = tri_ref[...]

    r = lax.broadcasted_iota(jnp.int32, (rows, blk), 0)
    c = lax.broadcasted_iota(jnp.int32, (rows, blk), 1)
    diag_mask = c < (r & (blk - 1))

    def kv(j):
        start = pl.multiple_of(j * blk, blk)
        return k_ref[0, pl.ds(start, blk), :], v_ref[0, pl.ds(start, blk), :]

    kd, vd = kv(qi)
    kp, vp = kv(jnp.maximum(qi - 1, 0))
    zd, spd, ltd = _stick_scores(qrows, kd, tri, diag_mask, False)
    zp, spp, ltp = _stick_scores(qrows, kp, tri, None, False)
    carry_d = _row_sum(spd)
    pv_d = _stick_values(zd, spd, ltd, jnp.zeros((rows, 1), F32), vd, diag_mask, False)
    pv_p = _stick_values(zp, spp, ltp, carry_d, vp, None, False)
    has_prev = qi > 0
    acc = pv_d + jnp.where(has_prev, pv_p, 0.0)
    carry = carry_d + jnp.where(has_prev, _row_sum(spp), 0.0)

    def live(carry):
        return jnp.min(carry) < DEAD_LOG2

    def cond(state):
        j, _, _, alive = state
        return jnp.logical_and(j >= 0, alive)

    def body(state):
        j, acc, carry, _ = state
        kblk, vblk = kv(j)
        z, sp, later = _stick_scores(qrows, kblk, tri, None, False)
        acc = acc + _stick_values(z, sp, later, carry, vblk, None, False)
        carry = carry + _row_sum(sp)
        return j - 1, acc, carry, live(carry)

    _, acc, _, _ = lax.while_loop(cond, body, (qi - 2, acc, carry, live(carry)))
    lane_o = lax.broadcasted_iota(jnp.int32, (blk, LANES), 1)
    o_ref[0] = jnp.where(lane_o < HEAD_DIM, acc[:blk], acc[blk:]).astype(o_ref.dtype)


def _tri(blk):
    j = lax.broadcasted_iota(jnp.int32, (blk, blk), 0)
    s = lax.broadcasted_iota(jnp.int32, (blk, blk), 1)
    t = (j > s).astype(BF16)
    return jnp.concatenate([t, t], axis=0)


def _attn_prompt(q, kb, vb, blk):
    b, t, _ = q.shape
    n_tiles = ATTN_WIDTH // LANES
    qspec = pl.BlockSpec((1, blk, LANES), lambda bi, hi, qi: (bi, qi, hi))
    kvspec = pl.BlockSpec((1, t, LANES), lambda bi, hi, qi: (bi, 0, hi))
    return pl.pallas_call(
        functools.partial(_attn_prompt_kernel, blk=blk),
        grid=(b, n_tiles, t // blk),
        in_specs=[qspec, kvspec, kvspec, _const_spec((2 * blk, blk))],
        out_specs=qspec,
        out_shape=jax.ShapeDtypeStruct((b, t, ATTN_WIDTH), BF16),
        compiler_params=pltpu.CompilerParams(
            dimension_semantics=("arbitrary", "arbitrary", "arbitrary"),
            vmem_limit_bytes=VMEM_LIMIT),
        name="attn_prompt",
    )(q, kb, vb, _tri(blk))


def _attn_sample_kernel(q_ref, kn_ref, vn_ref, ck_ref, cv_ref, tri_ref, o_ref, *, blk, new_blk):
    tq = q_ref.shape[1]
    rows = N_HEADS * tq
    q = q_ref[0]
    qrep = jnp.concatenate([q] * N_HEADS, axis=0)
    r = lax.broadcasted_iota(jnp.int32, (rows, ATTN_WIDTH), 0)
    c = lax.broadcasted_iota(jnp.int32, (rows, ATTN_WIDTH), 1)
    tq_bits = tq.bit_length() - 1
    head_bits = HEAD_DIM.bit_length() - 1
    assert tq == 1 << tq_bits
    head_sel = (r >> tq_bits) == (c >> head_bits)
    qrows = jnp.where(head_sel, qrep, jnp.zeros_like(qrep))
    tri = tri_ref[...]

    pad = jnp.zeros((new_blk - tq, ATTN_WIDTH), BF16)
    kblk = jnp.concatenate([kn_ref[0], pad], axis=0)
    vblk = jnp.concatenate([vn_ref[0], pad], axis=0)
    rn = lax.broadcasted_iota(jnp.int32, (rows, new_blk), 0)
    cn = lax.broadcasted_iota(jnp.int32, (rows, new_blk), 1)
    new_mask = cn < (rn & (tq - 1))
    tri_new = jnp.concatenate([tri[:new_blk, :new_blk], tri[:new_blk, :new_blk]], axis=0)
    z, sp, later = _stick_scores(qrows, kblk, tri_new, new_mask, False)
    acc = _stick_values(z, sp, later, jnp.zeros((rows, 1), F32), vblk, new_mask, False)
    carry = _row_sum(sp)

    past = ck_ref.shape[2]
    for j in reversed(range(past // blk)):
        kblk = ck_ref[0, :, j * blk:(j + 1) * blk].astype(BF16)
        vblk = cv_ref[0, :, j * blk:(j + 1) * blk].astype(BF16)
        z, sp, later = _stick_scores(qrows, kblk, tri, None, True)
        acc = acc + _stick_values(z, sp, later, carry, vblk, None, True)
        carry = carry + _row_sum(sp)

    co = lax.broadcasted_iota(jnp.int32, (tq, ATTN_WIDTH), 1) >> head_bits
    out = jnp.zeros((tq, ATTN_WIDTH), F32)
    for h in range(N_HEADS):
        out = jnp.where(co == h, acc[h * tq:(h + 1) * tq], out)
    o_ref[0] = out.astype(o_ref.dtype)


def _attn_sample(q, kb, vb, cache_k, cache_v, blk, new_blk):
    b, tq, _ = q.shape
    past = cache_k.shape[2]
    new = pl.BlockSpec((1, tq, ATTN_WIDTH), lambda bi: (bi, 0, 0))
    old = pl.BlockSpec((1, ATTN_WIDTH, past), lambda bi: (bi, 0, 0))
    return pl.pallas_call(
        functools.partial(_attn_sample_kernel, blk=blk, new_blk=new_blk),
        grid=(b,),
        in_specs=[new, new, new, old, old, _const_spec((2 * blk, blk))],
        out_specs=new,
        out_shape=jax.ShapeDtypeStruct((b, tq, ATTN_WIDTH), BF16),
        compiler_params=pltpu.CompilerParams(dimension_semantics=("arbitrary",),
                                             vmem_limit_bytes=VMEM_LIMIT),
        name="attn_sample",
    )(q, kb, vb, cache_k, cache_v, _tri(blk))


def _with_history(hist, cur):
    nb, h, c = hist.shape
    t = cur.shape[1]
    return jnp.concatenate([hist, cur], axis=1).reshape(nb * (h + t), c)


def _drop_history(flat, nb, h, t):
    c = flat.shape[-1]
    return flat.reshape(nb, h + t, c)[:, h:, :].reshape(nb * t, c)


def _gelu_tanh(x):
    return 0.5 * x * (1.0 + jnp.tanh(0.7978845608028654 * (x + 0.044715 * (x * x * x))))


def _post_kernel(x_ref, attn_ref, u_ref, ga_ref, gb_ref, pleft_ref, cleft_ref,
                 wa_ref, wb_ref, pw_ref, ps_ref, wo_ref, nffn_ref, wup_ref, cw_ref, cb_ref,
                 wdn_ref, nfin_ref,
                 y_ref, ptail_ref, ctail_ref,
                 uhist_ref, chist_ref, h2_ref, acc_ref, *, pos0):
    nb, tt, _ = x_ref.shape
    m = nb * tt
    ti = pl.program_id(1)

    @pl.when(ti == 0)
    def _():
        uhist_ref[...] = pleft_ref[...]
        chist_ref[...] = cleft_ref[...]

    x = x_ref[...].reshape(m, D_MODEL)
    u3 = u_ref[...]
    u = u3.reshape(m, POOL_WIDTH)

    ext = _with_history(uhist_ref[...], u3)
    row = lax.broadcasted_iota(jnp.int32, (m, POOL_GROUP), 0)
    pos = pos0 + ti * tt + (row & (tt - 1))
    pmix = []
    for g, w in enumerate(POOL_WINDOWS):
        sl = slice(g * POOL_GROUP, (g + 1) * POOL_GROUP)
        s = ext[:, sl]
        shift = 1
        while shift < w:
            s = s + pltpu.roll(s, shift, axis=0)
            shift *= 2
        win = _drop_history(s, nb, POOL_HALO, tt)
        cnt = jnp.minimum(pos + 1, w).astype(F32)
        pooled = win / cnt - u[:, sl]
        pmix.append(jnp.dot(pooled.astype(BF16), pw_ref[g], preferred_element_type=F32))
    pmix = jnp.concatenate(pmix, axis=1) * ps_ref[...]
    branch_b = jnp.dot(pmix.astype(BF16), wb_ref[...], preferred_element_type=F32)
    branch_a = jnp.dot(attn_ref[...].reshape(m, ATTN_WIDTH), wa_ref[...], preferred_element_type=F32)
    merged = (ga_ref[...].reshape(m, D_MODEL).astype(F32) * branch_a
              + gb_ref[...].reshape(m, D_MODEL).astype(F32) * branch_b)
    x1 = x + jnp.dot(merged.astype(BF16), wo_ref[...], preferred_element_type=F32)

    tail = u3[:, tt - POOL_HALO:, :]
    uhist_ref[...] = tail
    ptail_ref[...] = tail

    y_ref[...] = x1.reshape(nb, tt, D_MODEL)
    h2_ref[...] = _rmsnorm(x1, nffn_ref[...]).astype(BF16)
    n_chunks = D_FF // FF_CHUNK

    def up_proj(j):
        cols = [slice(half * D_FF + j * FF_CHUNK, half * D_FF + (j + 1) * FF_CHUNK) for half in range(2)]
        return [(cs, jnp.dot(h2_ref[...], wup_ref[:, cs], preferred_element_type=F32)) for cs in cols]

    ups = up_proj(0)
    for j in range(n_chunks):
        cur, ups = ups, (up_proj(j + 1) if j + 1 < n_chunks else None)
        halves = []
        for cs, up in cur:
            up3 = up.reshape(nb, tt, FF_CHUNK)
            e = _with_history(chist_ref[:, :, cs], up3)
            conv = cb_ref[:, cs] + up * cw_ref[2:3, cs]
            for d in (1, 2):
                shifted = _drop_history(pltpu.roll(e, d, axis=0), nb, CONV_HALO, tt)
                conv = conv + shifted * cw_ref[2 - d:3 - d, cs]
            halves.append(conv)
            ctile = up3[:, tt - CONV_HALO:, :]
            chist_ref[:, :, cs] = ctile
            ctail_ref[:, :, cs] = ctile
        act = (_gelu_tanh(halves[0]) * halves[1]).astype(BF16)
        down = jnp.dot(act, wdn_ref[j * FF_CHUNK:(j + 1) * FF_CHUNK, :], preferred_element_type=F32)
        if j == 0:
            acc_ref[...] = down
        else:
            acc_ref[...] += down
    y = _rmsnorm(y_ref[...].reshape(m, D_MODEL) + acc_ref[...], nfin_ref[...])
    y_ref[...] = y.reshape(nb, tt, D_MODEL)


def _post(x, attn, u, ga, gb, pool_left, conv_left, weights, nb, tt, pos0):
    b, t, _ = x.shape
    (w_a, w_b, pool_w, pool_scale, w_o, norm_ffn, w_up, conv_w, conv_b, w_down, norm_final) = weights
    tile = lambda w: pl.BlockSpec((nb, tt, w), lambda bi, ti: (bi, ti, 0))
    per_seq = lambda r, w: pl.BlockSpec((nb, r, w), lambda bi, ti: (bi, 0, 0))
    consts = [w_a, w_b, pool_w, pool_scale, w_o, norm_ffn, w_up, conv_w, conv_b, w_down, norm_final]
    return pl.pallas_call(
        functools.partial(_post_kernel, pos0=pos0),
        grid=(b // nb, t // tt),
        in_specs=[tile(D_MODEL), tile(ATTN_WIDTH), tile(POOL_WIDTH), tile(D_MODEL), tile(D_MODEL),
                  per_seq(POOL_HALO, POOL_WIDTH), per_seq(CONV_HALO, 2 * D_FF)]
                 + [_const_spec(c.shape) for c in consts],
        out_specs=[tile(D_MODEL), per_seq(POOL_HALO, POOL_WIDTH), per_seq(CONV_HALO, 2 * D_FF)],
        out_shape=[jax.ShapeDtypeStruct((b, t, D_MODEL), F32),
                   jax.ShapeDtypeStruct((b, POOL_HALO, POOL_WIDTH), F32),
                   jax.ShapeDtypeStruct((b, CONV_HALO, 2 * D_FF), F32)],
        scratch_shapes=[pltpu.VMEM((nb, POOL_HALO, POOL_WIDTH), F32),
                        pltpu.VMEM((nb, CONV_HALO, 2 * D_FF), F32),
                        pltpu.VMEM((nb * tt, D_MODEL), BF16),
                        pltpu.VMEM((nb * tt, D_MODEL), F32)],
        compiler_params=pltpu.CompilerParams(dimension_semantics=("arbitrary", "arbitrary"),
                                             vmem_limit_bytes=VMEM_LIMIT),
        name="post",
    )(x, attn, u, ga, gb, pool_left, conv_left, *consts)


def _left_pad(state, rows):
    b, r, c = state.shape
    return jnp.concatenate([jnp.zeros((b, rows - r, c), state.dtype), state], axis=1)


def _stream(x, past, pool_left, conv_left, norm_mix, w_in_bf, post_weights, *, tm, nb, tt, pos0):
    b, t, _ = x.shape
    m = b * t
    kv_transposed = past is None
    q, k, v, u, kb, vb, ga, gb = _inproj(x.reshape(m, D_MODEL), norm_mix, w_in_bf, tm, t, kv_transposed)
    seq = lambda a: a.reshape(b, t, a.shape[-1])
    if past is None:
        attn = _attn_prompt(seq(q), seq(kb), seq(vb), blk=256)
        heads = lambda a: a.reshape(1, b, N_HEADS, HEAD_DIM, t).transpose(0, 1, 4, 2, 3)
    else:
        time_last = lambda c: c.transpose(0, 2, 3, 1).reshape(b, ATTN_WIDTH, c.shape[1])
        attn = _attn_sample(seq(q), seq(kb), seq(vb), time_last(past[0]), time_last(past[1]),
                            blk=256, new_blk=128)
        heads = lambda a: a.reshape(1, b, t, N_HEADS, HEAD_DIM)
    y, ptail, ctail = _post(x, attn, seq(u), seq(ga), seq(gb),
                            _left_pad(pool_left, POOL_HALO), _left_pad(conv_left, CONV_HALO),
                            post_weights, nb, tt, pos0)
    return (y, heads(k), heads(v),
            ptail[None, :, POOL_HALO - POOL_STATE:, :],
            ctail[None, :, CONV_HALO - (CONV_WIDTH - 1):, :])


def kernel(x_prompt, x_sample, cache_k, cache_v, state_pool, state_conv, norm_mix, w_in, w_a, w_b,
           pool_w, pool_scale, w_o, norm_ffn, w_up, conv_w, conv_b, w_down, norm_final):
    assert w_in.shape[0] == 1, "single-layer trunk"
    row = lambda a: a.reshape(1, -1)
    post_weights = (w_a[0].astype(BF16), w_b[0].astype(BF16), pool_w[0].astype(BF16),
                    row(pool_scale[0]), w_o[0].astype(BF16), row(norm_ffn[0]),
                    w_up[0].astype(BF16), conv_w[0], row(conv_b[0]), w_down[0].astype(BF16),
                    row(norm_final))
    w_in_bf = w_in[0].astype(BF16)
    bp = x_prompt.shape[0]
    zeros_pool = jnp.zeros((bp, POOL_STATE, POOL_WIDTH), F32)
    zeros_conv = jnp.zeros((bp, CONV_WIDTH - 1, 2 * D_FF), F32)
    yp, kp, vp, pp, cp = _stream(x_prompt, None, zeros_pool, zeros_conv, norm_mix[0], w_in_bf,
                                 post_weights, tm=512, nb=1, tt=512, pos0=0)
    bs, ts, _ = x_sample.shape
    ys, ks, vs, ps, cs = _stream(x_sample, (cache_k[0], cache_v[0]), state_pool[0], state_conv[0],
                                 norm_mix[0], w_in_bf, post_weights,
                                 tm=bs * ts, nb=bs, tt=ts, pos0=cache_k.shape[2])
    return (yp, ys, kp, vp, pp, cp, ks, vs, ps, cs)
```

```python
import functools

import jax
import jax.numpy as jnp
from jax import lax
from jax.experimental import pallas as pl
from jax.experimental.pallas import tpu as pltpu

D_MODEL = 1024
N_HEADS = 8
HEAD_DIM = 64
ATTN_WIDTH = N_HEADS * HEAD_DIM
POOL_WINDOWS = (2, 4, 8, 16)
POOL_WIDTH = 512
POOL_GROUP = POOL_WIDTH // len(POOL_WINDOWS)
POOL_STATE = max(POOL_WINDOWS) - 1
D_FF = 2816
CONV_WIDTH = 3
EPS = 1e-6
IN_WIDTH = 3 * ATTN_WIDTH + POOL_WIDTH + 2 * D_MODEL

LANES = 128
HEADS_PER_LANE_TILE = LANES // HEAD_DIM
POOL_HALO = 16
CONV_HALO = 8
FF_CHUNK = 256
VMEM_LIMIT = 56 * 1024 * 1024
LOG2E = 1.4426950408889634
DEAD_LOG2 = 160.0
SOFTPLUS_CLAMP = 126.0

F32 = jnp.float32
BF16 = jnp.bfloat16


def _rmsnorm(x, g):
    return x * lax.rsqrt(jnp.mean(x * x, axis=-1, keepdims=True) + EPS) * g


def _const_spec(shape):
    nd = len(shape)
    return pl.BlockSpec(shape, lambda *_: (0,) * nd, pipeline_mode=pl.Buffered(1))


def _inproj_kernel(x_ref, g_ref, w_ref, q_ref, k_ref, v_ref, u_ref, kb_ref, vb_ref, ga_ref, gb_ref,
                   *, kv_transposed):
    h = _rmsnorm(x_ref[...], g_ref[...]).astype(BF16)

    def proj(lo, hi):
        return jnp.dot(h, w_ref[:, lo:hi], preferred_element_type=F32)

    a = ATTN_WIDTH
    q_ref[...] = (proj(0, a) * (HEAD_DIM ** -0.5 * LOG2E)).astype(BF16)
    k = proj(a, 2 * a)
    kb_ref[...] = k.astype(BF16)
    v = proj(2 * a, 3 * a)
    vb_ref[...] = v.astype(BF16)
    if kv_transposed:
        k_ref[0] = k.T
        v_ref[0] = v.T
    else:
        k_ref[...] = k
        v_ref[...] = v
    u0 = 3 * a
    u_ref[...] = proj(u0, u0 + POOL_WIDTH)
    g0 = u0 + POOL_WIDTH
    ga_ref[...] = jax.nn.sigmoid(proj(g0, g0 + D_MODEL)).astype(BF16)
    gb_ref[...] = jax.nn.sigmoid(proj(g0 + D_MODEL, g0 + 2 * D_MODEL)).astype(BF16)


def _inproj(x2d, norm_mix, w_in_bf, tm, seq_len, kv_transposed):
    m = x2d.shape[0]
    row = lambda w: pl.BlockSpec((tm, w), lambda i: (i, 0))
    outs = [(ATTN_WIDTH, BF16), (ATTN_WIDTH, F32), (ATTN_WIDTH, F32), (POOL_WIDTH, F32),
            (ATTN_WIDTH, BF16), (ATTN_WIDTH, BF16), (D_MODEL, BF16), (D_MODEL, BF16)]
    out_specs = [row(w) for w, _ in outs]
    out_shape = [jax.ShapeDtypeStruct((m, w), dt) for w, dt in outs]
    if kv_transposed:
        tiles = seq_len // tm
        for i in (1, 2):
            out_specs[i] = pl.BlockSpec((1, ATTN_WIDTH, tm), lambda i: (i // tiles, 0, i % tiles))
            out_shape[i] = jax.ShapeDtypeStruct((m // seq_len, ATTN_WIDTH, seq_len), F32)
    return pl.pallas_call(
        functools.partial(_inproj_kernel, kv_transposed=kv_transposed),
        grid=(m // tm,),
        in_specs=[row(D_MODEL), _const_spec((1, D_MODEL)), _const_spec((D_MODEL, IN_WIDTH))],
        out_specs=out_specs,
        out_shape=out_shape,
        compiler_params=pltpu.CompilerParams(dimension_semantics=("arbitrary",),
                                             vmem_limit_bytes=VMEM_LIMIT),
        name="inproj",
    )(x2d, norm_mix.reshape(1, D_MODEL), w_in_bf)


_NT = (((1,), (1,)), ((), ()))


def _softplus2(z):
    return jnp.maximum(jnp.log2(1.0 + jnp.exp2(jnp.minimum(z, SOFTPLUS_CLAMP))), z)


def _stick_scores(qrows, kblk, tri, mask, k_transposed):
    if k_transposed:
        z = jnp.dot(qrows, kblk, preferred_element_type=F32)
    else:
        z = lax.dot_general(qrows, kblk, _NT, preferred_element_type=F32)
    sp = _softplus2(z)
    if mask is not None:
        sp = jnp.where(mask, sp, 0.0)
    hi = sp.astype(BF16)
    lo = (sp - hi.astype(F32)).astype(BF16)
    upto = jnp.dot(jnp.concatenate([hi, lo], axis=1), tri, preferred_element_type=F32)
    return z, sp, upto


def _stick_values(z, upto, carry, vblk, mask, v_transposed):
    a = jnp.exp2(z - upto - carry)
    if mask is not None:
        a = jnp.where(mask, a, 0.0)
    a = a.astype(BF16)
    if v_transposed:
        return lax.dot_general(a, vblk, _NT, preferred_element_type=F32)
    return jnp.dot(a, vblk, preferred_element_type=F32)


def _row_sum(sp):
    return jnp.sum(sp, axis=1, keepdims=True)


def _attn_prompt_kernel(q_ref, k_ref, v_ref, tri_ref, o_ref, *, blk):
    qi = pl.program_id(1)
    n_tiles = q_ref.shape[2] // LANES
    rows = 2 * blk
    tri = tri_ref[...]
    lower = lax.broadcasted_iota(jnp.int32, (blk, LANES), 1) < HEAD_DIM
    r = lax.broadcasted_iota(jnp.int32, (rows, blk), 0)
    c = lax.broadcasted_iota(jnp.int32, (rows, blk), 1)
    diag_mask = c < (r & (blk - 1))
    no_carry = jnp.zeros((rows, 1), F32)
    has_prev = qi > 0

    def q_rows(t):
        q2 = q_ref[0, :, t * LANES:(t + 1) * LANES]
        zero = jnp.zeros_like(q2)
        return jnp.concatenate([jnp.where(lower, q2, zero), jnp.where(lower, zero, q2)], axis=0)

    def kv(j, t):
        start = pl.multiple_of(j * blk, blk)
        lanes = slice(t * LANES, (t + 1) * LANES)
        return k_ref[0, pl.ds(start, blk), lanes], v_ref[0, pl.ds(start, blk), lanes]

    qs, accs, carries = [], [], []
    for t in range(n_tiles):
        q = q_rows(t)
        kd, vd = kv(qi, t)
        kp, vp = kv(jnp.maximum(qi - 1, 0), t)
        zd, spd, upd = _stick_scores(q, kd, tri, diag_mask, False)
        zp, spp, upp = _stick_scores(q, kp, tri, None, False)
        carry_d = _row_sum(spd)
        pv_d = _stick_values(zd, upd, no_carry, vd, diag_mask, False)
        pv_p = _stick_values(zp, upp, carry_d, vp, None, False)
        qs.append(q)
        accs.append(pv_d + jnp.where(has_prev, pv_p, 0.0))
        carries.append(carry_d + jnp.where(has_prev, _row_sum(spp), 0.0))

    def live(carries):
        return jnp.min(functools.reduce(jnp.minimum, carries)) < DEAD_LOG2

    def cond(state):
        j, _, _, alive = state
        return jnp.logical_and(j >= 0, alive)

    def body(state):
        j, accs, carries, _ = state
        new_accs, new_carries = [], []
        for t in range(n_tiles):
            kblk, vblk = kv(j, t)
            z, sp, upto = _stick_scores(qs[t], kblk, tri, None, False)
            new_accs.append(accs[t] + _stick_values(z, upto, carries[t], vblk, None, False))
            new_carries.append(carries[t] + _row_sum(sp))
        return j - 1, tuple(new_accs), tuple(new_carries), live(new_carries)

    _, accs, _, _ = lax.while_loop(cond, body, (qi - 2, tuple(accs), tuple(carries), live(carries)))
    for t in range(n_tiles):
        o_ref[0, :, t * LANES:(t + 1) * LANES] = jnp.where(
            lower, accs[t][:blk], accs[t][blk:]).astype(o_ref.dtype)


def _tri(blk):
    j = lax.broadcasted_iota(jnp.int32, (blk, blk), 0)
    s = lax.broadcasted_iota(jnp.int32, (blk, blk), 1)
    t = (j >= s).astype(BF16)
    return jnp.concatenate([t, t], axis=0)


def _attn_prompt(q, kb, vb, blk):
    b, t, _ = q.shape
    qspec = pl.BlockSpec((1, blk, ATTN_WIDTH), lambda bi, qi: (bi, qi, 0))
    kvspec = pl.BlockSpec((1, t, ATTN_WIDTH), lambda bi, qi: (bi, 0, 0))
    return pl.pallas_call(
        functools.partial(_attn_prompt_kernel, blk=blk),
        grid=(b, t // blk),
        in_specs=[qspec, kvspec, kvspec, _const_spec((2 * blk, blk))],
        out_specs=qspec,
        out_shape=jax.ShapeDtypeStruct((b, t, ATTN_WIDTH), BF16),
        compiler_params=pltpu.CompilerParams(
            dimension_semantics=("arbitrary", "arbitrary"),
            vmem_limit_bytes=VMEM_LIMIT),
        name="attn_prompt",
    )(q, kb, vb, _tri(blk))


def _attn_sample_kernel(q_ref, kn_ref, vn_ref, ck_ref, cv_ref, tri_ref, o_ref, *, blk, new_blk):
    tq = q_ref.shape[1]
    rows = N_HEADS * tq
    q = q_ref[0]
    qrep = jnp.concatenate([q] * N_HEADS, axis=0)
    r = lax.broadcasted_iota(jnp.int32, (rows, ATTN_WIDTH), 0)
    c = lax.broadcasted_iota(jnp.int32, (rows, ATTN_WIDTH), 1)
    tq_bits = tq.bit_length() - 1
    head_bits = HEAD_DIM.bit_length() - 1
    assert tq == 1 << tq_bits
    head_sel = (r >> tq_bits) == (c >> head_bits)
    qrows = jnp.where(head_sel, qrep, jnp.zeros_like(qrep))
    tri = tri_ref[...]

    pad = jnp.zeros((new_blk - tq, ATTN_WIDTH), BF16)
    kblk = jnp.concatenate([kn_ref[0], pad], axis=0)
    vblk = jnp.concatenate([vn_ref[0], pad], axis=0)
    rn = lax.broadcasted_iota(jnp.int32, (rows, new_blk), 0)
    cn = lax.broadcasted_iota(jnp.int32, (rows, new_blk), 1)
    new_mask = cn < (rn & (tq - 1))
    tri_new = jnp.concatenate([tri[:new_blk, :new_blk], tri[:new_blk, :new_blk]], axis=0)
    z, sp, upto = _stick_scores(qrows, kblk, tri_new, new_mask, False)
    acc = _stick_values(z, upto, jnp.zeros((rows, 1), F32), vblk, new_mask, False)
    carry = _row_sum(sp)

    past = ck_ref.shape[2]
    for j in reversed(range(past // blk)):
        kblk = ck_ref[0, :, j * blk:(j + 1) * blk].astype(BF16)
        vblk = cv_ref[0, :, j * blk:(j + 1) * blk].astype(BF16)
        z, sp, upto = _stick_scores(qrows, kblk, tri, None, True)
        acc = acc + _stick_values(z, upto, carry, vblk, None, True)
        carry = carry + _row_sum(sp)

    co = lax.broadcasted_iota(jnp.int32, (tq, ATTN_WIDTH), 1) >> head_bits
    out = jnp.zeros((tq, ATTN_WIDTH), F32)
    for h in range(N_HEADS):
        out = jnp.where(co == h, acc[h * tq:(h + 1) * tq], out)
    o_ref[0] = out.astype(o_ref.dtype)


def _attn_sample(q, kb, vb, cache_k, cache_v, blk, new_blk):
    b, tq, _ = q.shape
    past = cache_k.shape[2]
    new = pl.BlockSpec((1, tq, ATTN_WIDTH), lambda bi: (bi, 0, 0))
    old = pl.BlockSpec((1, ATTN_WIDTH, past), lambda bi: (bi, 0, 0))
    return pl.pallas_call(
        functools.partial(_attn_sample_kernel, blk=blk, new_blk=new_blk),
        grid=(b,),
        in_specs=[new, new, new, old, old, _const_spec((2 * blk, blk))],
        out_specs=new,
        out_shape=jax.ShapeDtypeStruct((b, tq, ATTN_WIDTH), BF16),
        compiler_params=pltpu.CompilerParams(dimension_semantics=("arbitrary",),
                                             vmem_limit_bytes=VMEM_LIMIT),
        name="attn_sample",
    )(q, kb, vb, cache_k, cache_v, _tri(blk))


def _with_history(hist, cur):
    nb, h, c = hist.shape
    t = cur.shape[1]
    return jnp.concatenate([hist, cur], axis=1).reshape(nb * (h + t), c)


def _drop_history(flat, nb, h, t):
    c = flat.shape[-1]
    return flat.reshape(nb, h + t, c)[:, h:, :].reshape(nb * t, c)


def _gelu_tanh(x):
    return 0.5 * x * (1.0 + jnp.tanh(0.7978845608028654 * (x + 0.044715 * (x * x * x))))


def _post_kernel(x_ref, attn_ref, u_ref, ga_ref, gb_ref, pleft_ref, cleft_ref,
                 wa_ref, wb_ref, pw_ref, ps_ref, wo_ref, nffn_ref, wup_ref, cw_ref, cb_ref,
                 wdn_ref, nfin_ref,
                 y_ref, ptail_ref, ctail_ref,
                 uhist_ref, chist_ref, h2_ref, act_ref, *, pos0):
    nb, tt, _ = x_ref.shape
    m = nb * tt
    ti = pl.program_id(1)

    @pl.when(ti == 0)
    def _():
        uhist_ref[...] = pleft_ref[...]
        chist_ref[...] = cleft_ref[...]

    x = x_ref[...].reshape(m, D_MODEL)
    u3 = u_ref[...]
    u = u3.reshape(m, POOL_WIDTH)

    ext = _with_history(uhist_ref[...], u3)
    row = lax.broadcasted_iota(jnp.int32, (m, POOL_GROUP), 0)
    pos = pos0 + ti * tt + (row & (tt - 1))
    pmix = []
    for g, w in enumerate(POOL_WINDOWS):
        sl = slice(g * POOL_GROUP, (g + 1) * POOL_GROUP)
        s = ext[:, sl]
        shift = 1
        while shift < w:
            s = s + pltpu.roll(s, shift, axis=0)
            shift *= 2
        win = _drop_history(s, nb, POOL_HALO, tt)
        cnt = jnp.minimum(pos + 1, w).astype(F32)
        pooled = win / cnt - u[:, sl]
        pmix.append(jnp.dot(pooled.astype(BF16), pw_ref[g], preferred_element_type=F32))
    pmix = jnp.concatenate(pmix, axis=1) * ps_ref[...]
    branch_b = jnp.dot(pmix.astype(BF16), wb_ref[...], preferred_element_type=F32)
    branch_a = jnp.dot(attn_ref[...].reshape(m, ATTN_WIDTH), wa_ref[...], preferred_element_type=F32)
    merged = (ga_ref[...].reshape(m, D_MODEL).astype(F32) * branch_a
              + gb_ref[...].reshape(m, D_MODEL).astype(F32) * branch_b)
    x1 = x + jnp.dot(merged.astype(BF16), wo_ref[...], preferred_element_type=F32)

    tail = u3[:, tt - POOL_HALO:, :]
    uhist_ref[...] = tail
    ptail_ref[...] = tail

    y_ref[...] = x1.reshape(nb, tt, D_MODEL)
    h2_ref[...] = _rmsnorm(x1, nffn_ref[...]).astype(BF16)
    n_chunks = D_FF // FF_CHUNK

    def up_proj(j):
        cols = [slice(half * D_FF + j * FF_CHUNK, half * D_FF + (j + 1) * FF_CHUNK) for half in range(2)]
        return [(cs, jnp.dot(h2_ref[...], wup_ref[:, cs], preferred_element_type=F32)) for cs in cols]

    ups = up_proj(0)
    for j in range(n_chunks):
        cur, ups = ups, (up_proj(j + 1) if j + 1 < n_chunks else None)
        halves = []
        for cs, up in cur:
            up3 = up.reshape(nb, tt, FF_CHUNK)
            e = _with_history(chist_ref[:, :, cs], up3)
            conv = cb_ref[:, cs] + up * cw_ref[2:3, cs]
            for d in (1, 2):
                shifted = _drop_history(pltpu.roll(e, d, axis=0), nb, CONV_HALO, tt)
                conv = conv + shifted * cw_ref[2 - d:3 - d, cs]
            halves.append(conv)
            ctile = up3[:, tt - CONV_HALO:, :]
            chist_ref[:, :, cs] = ctile
            ctail_ref[:, :, cs] = ctile
        act = (_gelu_tanh(halves[0]) * halves[1]).astype(BF16)
        act_ref[:, j * FF_CHUNK:(j + 1) * FF_CHUNK] = act
    down = jnp.dot(act_ref[...], wdn_ref[...], preferred_element_type=F32)
    y = _rmsnorm(y_ref[...].reshape(m, D_MODEL) + down, nfin_ref[...])
    y_ref[...] = y.reshape(nb, tt, D_MODEL)


def _post(x, attn, u, ga, gb, pool_left, conv_left, weights, nb, tt, pos0):
    b, t, _ = x.shape
    (w_a, w_b, pool_w, pool_scale, w_o, norm_ffn, w_up, conv_w, conv_b, w_down, norm_final) = weights
    tile = lambda w: pl.BlockSpec((nb, tt, w), lambda bi, ti: (bi, ti, 0))
    per_seq = lambda r, w: pl.BlockSpec((nb, r, w), lambda bi, ti: (bi, 0, 0))
    consts = [w_a, w_b, pool_w, pool_scale, w_o, norm_ffn, w_up, conv_w, conv_b, w_down, norm_final]
    return pl.pallas_call(
        functools.partial(_post_kernel, pos0=pos0),
        grid=(b // nb, t // tt),
        in_specs=[tile(D_MODEL), tile(ATTN_WIDTH), tile(POOL_WIDTH), tile(D_MODEL), tile(D_MODEL),
                  per_seq(POOL_HALO, POOL_WIDTH), per_seq(CONV_HALO, 2 * D_FF)]
                 + [_const_spec(c.shape) for c in consts],
        out_specs=[tile(D_MODEL), per_seq(POOL_HALO, POOL_WIDTH), per_seq(CONV_HALO, 2 * D_FF)],
        out_shape=[jax.ShapeDtypeStruct((b, t, D_MODEL), F32),
                   jax.ShapeDtypeStruct((b, POOL_HALO, POOL_WIDTH), F32),
                   jax.ShapeDtypeStruct((b, CONV_HALO, 2 * D_FF), F32)],
        scratch_shapes=[pltpu.VMEM((nb, POOL_HALO, POOL_WIDTH), F32),
                        pltpu.VMEM((nb, CONV_HALO, 2 * D_FF), F32),
                        pltpu.VMEM((nb * tt, D_MODEL), BF16),
                        pltpu.VMEM((nb * tt, D_FF), BF16)],
        compiler_params=pltpu.CompilerParams(dimension_semantics=("arbitrary", "arbitrary"),
                                             vmem_limit_bytes=VMEM_LIMIT),
        name="post",
    )(x, attn, u, ga, gb, pool_left, conv_left, *consts)


def _left_pad(state, rows):
    b, r, c = state.shape
    return jnp.concatenate([jnp.zeros((b, rows - r, c), state.dtype), state], axis=1)


def _stream(x, past, pool_left, conv_left, norm_mix, w_in_bf, post_weights, *, tm, nb, tt, pos0):
    b, t, _ = x.shape
    m = b * t
    kv_transposed = past is None
    q, k, v, u, kb, vb, ga, gb = _inproj(x.reshape(m, D_MODEL), norm_mix, w_in_bf, tm, t, kv_transposed)
    seq = lambda a: a.reshape(b, t, a.shape[-1])
    if past is None:
        attn = _attn_prompt(seq(q), seq(kb), seq(vb), blk=256)
        heads = lambda a: a.reshape(1, b, N_HEADS, HEAD_DIM, t).transpose(0, 1, 4, 2, 3)
    else:
        time_last = lambda c: c.transpose(0, 2, 3, 1).reshape(b, ATTN_WIDTH, c.shape[1])
        attn = _attn_sample(seq(q), seq(kb), seq(vb), time_last(past[0]), time_last(past[1]),
                            blk=256, new_blk=128)
        heads = lambda a: a.reshape(1, b, t, N_HEADS, HEAD_DIM)
    y, ptail, ctail = _post(x, attn, seq(u), seq(ga), seq(gb),
                            _left_pad(pool_left, POOL_HALO), _left_pad(conv_left, CONV_HALO),
                            post_weights, nb, tt, pos0)
    return (y, heads(k), heads(v),
            ptail[None, :, POOL_HALO - POOL_STATE:, :],
            ctail[None, :, CONV_HALO - (CONV_WIDTH - 1):, :])


def kernel(x_prompt, x_sample, cache_k, cache_v, state_pool, state_conv, norm_mix, w_in, w_a, w_b,
           pool_w, pool_scale, w_o, norm_ffn, w_up, conv_w, conv_b, w_down, norm_final):
    assert w_in.shape[0] == 1, "single-layer trunk"
    row = lambda a: a.reshape(1, -1)
    post_weights = (w_a[0].astype(BF16), w_b[0].astype(BF16), pool_w[0].astype(BF16),
                    row(pool_scale[0]), w_o[0].astype(BF16), row(norm_ffn[0]),
                    w_up[0].astype(BF16), conv_w[0], row(conv_b[0]), w_down[0].astype(BF16),
                    row(norm_final))
    w_in_bf = w_in[0].astype(BF16)
    bp = x_prompt.shape[0]
    zeros_pool = jnp.zeros((bp, POOL_STATE, POOL_WIDTH), F32)
    zeros_conv = jnp.zeros((bp, CONV_WIDTH - 1, 2 * D_FF), F32)
    yp, kp, vp, pp, cp = _stream(x_prompt, None, zeros_pool, zeros_conv, norm_mix[0], w_in_bf,
                                 post_weights, tm=512, nb=1, tt=512, pos0=0)
    bs, ts, _ = x_sample.shape
    ys, ks, vs, ps, cs = _stream(x_sample, (cache_k[0], cache_v[0]), state_pool[0], state_conv[0],
                                 norm_mix[0], w_in_bf, post_weights,
                                 tm=bs * ts, nb=bs, tt=ts, pos0=cache_k.shape[2])
    return (yp, ys, kp, vp, pp, cp, ks, vs, ps, cs)
```

```python
import functools

import jax
import jax.numpy as jnp
from jax import lax
from jax.experimental import pallas as pl
from jax.experimental.pallas import tpu as pltpu

D_MODEL = 1024
N_HEADS = 8
HEAD_DIM = 64
ATTN_WIDTH = N_HEADS * HEAD_DIM
POOL_WINDOWS = (2, 4, 8, 16)
POOL_WIDTH = 512
POOL_GROUP = POOL_WIDTH // len(POOL_WINDOWS)
POOL_STATE = max(POOL_WINDOWS) - 1
D_FF = 2816
CONV_WIDTH = 3
EPS = 1e-6
IN_WIDTH = 3 * ATTN_WIDTH + POOL_WIDTH + 2 * D_MODEL

LANES = 128
HEADS_PER_LANE_TILE = LANES // HEAD_DIM
POOL_HALO = 16
CONV_HALO = 8
FF_CHUNK = 256
VMEM_LIMIT = 56 * 1024 * 1024
LOG2E = 1.4426950408889634
DEAD_LOG2 = 160.0
SOFTPLUS_CLAMP = 126.0

F32 = jnp.float32
BF16 = jnp.bfloat16


def _rmsnorm(x, g):
    return x * lax.rsqrt(jnp.mean(x * x, axis=-1, keepdims=True) + EPS) * g


def _const_spec(shape):
    nd = len(shape)
    return pl.BlockSpec(shape, lambda *_: (0,) * nd, pipeline_mode=pl.Buffered(1))


def _inproj_kernel(x_ref, g_ref, w_ref, q_ref, k_ref, v_ref, u_ref, kb_ref, vb_ref, ga_ref, gb_ref,
                   *, kv_transposed):
    h = _rmsnorm(x_ref[...], g_ref[...]).astype(BF16)

    def proj(lo, hi):
        return jnp.dot(h, w_ref[:, lo:hi], preferred_element_type=F32)

    a = ATTN_WIDTH
    u0 = 3 * a
    g0 = u0 + POOL_WIDTH
    ga_ref[...] = jax.nn.sigmoid(proj(g0, g0 + D_MODEL)).astype(BF16)
    gb_ref[...] = jax.nn.sigmoid(proj(g0 + D_MODEL, g0 + 2 * D_MODEL)).astype(BF16)
    k = proj(a, 2 * a)
    kb_ref[...] = k.astype(BF16)
    v = proj(2 * a, 3 * a)
    vb_ref[...] = v.astype(BF16)
    if kv_transposed:
        k_ref[0] = k.T
        v_ref[0] = v.T
    else:
        k_ref[...] = k
        v_ref[...] = v
    q_ref[...] = (proj(0, a) * (HEAD_DIM ** -0.5 * LOG2E)).astype(BF16)
    u_ref[...] = proj(u0, u0 + POOL_WIDTH)


def _inproj(x2d, norm_mix, w_in_bf, tm, seq_len, kv_transposed):
    m = x2d.shape[0]
    row = lambda w: pl.BlockSpec((tm, w), lambda i: (i, 0))
    outs = [(ATTN_WIDTH, BF16), (ATTN_WIDTH, F32), (ATTN_WIDTH, F32), (POOL_WIDTH, F32),
            (ATTN_WIDTH, BF16), (ATTN_WIDTH, BF16), (D_MODEL, BF16), (D_MODEL, BF16)]
    out_specs = [row(w) for w, _ in outs]
    out_shape = [jax.ShapeDtypeStruct((m, w), dt) for w, dt in outs]
    if kv_transposed:
        tiles = seq_len // tm
        for i in (1, 2):
            out_specs[i] = pl.BlockSpec((1, ATTN_WIDTH, tm), lambda i: (i // tiles, 0, i % tiles))
            out_shape[i] = jax.ShapeDtypeStruct((m // seq_len, ATTN_WIDTH, seq_len), F32)
    return pl.pallas_call(
        functools.partial(_inproj_kernel, kv_transposed=kv_transposed),
        grid=(m // tm,),
        in_specs=[row(D_MODEL), _const_spec((1, D_MODEL)), _const_spec((D_MODEL, IN_WIDTH))],
        out_specs=out_specs,
        out_shape=out_shape,
        compiler_params=pltpu.CompilerParams(dimension_semantics=("arbitrary",),
                                             vmem_limit_bytes=VMEM_LIMIT),
        name="inproj",
    )(x2d, norm_mix.reshape(1, D_MODEL), w_in_bf)


_NT = (((1,), (1,)), ((), ()))


def _softplus2(z):
    return jnp.maximum(jnp.log2(1.0 + jnp.exp2(jnp.minimum(z, SOFTPLUS_CLAMP))), z)


def _stick_scores(qrows, kblk, tri, mask, k_transposed):
    if k_transposed:
        z = jnp.dot(qrows, kblk, preferred_element_type=F32)
    else:
        z = lax.dot_general(qrows, kblk, _NT, preferred_element_type=F32)
    sp = _softplus2(z)
    if mask is not None:
        sp = jnp.where(mask, sp, 0.0)
    hi = sp.astype(BF16)
    lo = (sp - hi.astype(F32)).astype(BF16)
    upto = jnp.dot(jnp.concatenate([hi, lo], axis=1), tri, preferred_element_type=F32)
    return z, sp, upto


def _stick_values(z, upto, carry, vblk, mask, v_transposed):
    a = jnp.exp2(z - upto - carry)
    if mask is not None:
        a = jnp.where(mask, a, 0.0)
    a = a.astype(BF16)
    if v_transposed:
        return lax.dot_general(a, vblk, _NT, preferred_element_type=F32)
    return jnp.dot(a, vblk, preferred_element_type=F32)


def _row_sum(sp):
    return jnp.sum(sp, axis=1, keepdims=True)


def _attn_prompt_kernel(q_ref, k_ref, v_ref, tri_ref, o_ref, *, blk):
    n_q = q_ref.shape[1] // blk
    n_tiles = q_ref.shape[2] // LANES
    rows = 2 * blk
    tri = tri_ref[...]
    lower = lax.broadcasted_iota(jnp.int32, (blk, LANES), 1) < HEAD_DIM
    r = lax.broadcasted_iota(jnp.int32, (rows, blk), 0)
    c = lax.broadcasted_iota(jnp.int32, (rows, blk), 1)
    diag_mask = c < (r & (blk - 1))
    no_carry = jnp.zeros((rows, 1), F32)

    def q_rows(b, t):
        q2 = q_ref[0, b * blk:(b + 1) * blk, t * LANES:(t + 1) * LANES]
        zero = jnp.zeros_like(q2)
        return jnp.concatenate([jnp.where(lower, q2, zero), jnp.where(lower, zero, q2)], axis=0)

    def kv(j, t):
        start = pl.multiple_of(j * blk, blk)
        lanes = slice(t * LANES, (t + 1) * LANES)
        return k_ref[0, pl.ds(start, blk), lanes], v_ref[0, pl.ds(start, blk), lanes]

    def newest_two(b):
        qi = pl.program_id(1) * n_q + b
        has_prev = qi > 0
        qs, accs, carries = [], [], []
        for t in range(n_tiles):
            q = q_rows(b, t)
            kd, vd = kv(qi, t)
            kp, vp = kv(jnp.maximum(qi - 1, 0), t)
            zd, spd, upd = _stick_scores(q, kd, tri, diag_mask, False)
            zp, spp, upp = _stick_scores(q, kp, tri, None, False)
            carry_d = _row_sum(spd)
            pv_d = _stick_values(zd, upd, no_carry, vd, diag_mask, False)
            pv_p = _stick_values(zp, upp, carry_d, vp, None, False)
            qs.append(q)
            accs.append(pv_d + jnp.where(has_prev, pv_p, 0.0))
            carries.append(carry_d + jnp.where(has_prev, _row_sum(spp), 0.0))
        return qi, qs, accs, carries

    def live(carries):
        return jnp.min(functools.reduce(jnp.minimum, carries)) < DEAD_LOG2

    def older(qi, qs, accs, carries):
        def cond(state):
            j, _, _, alive = state
            return jnp.logical_and(j >= 0, alive)

        def body(state):
            j, accs, carries, _ = state
            new_accs, new_carries = [], []
            for t in range(n_tiles):
                kblk, vblk = kv(j, t)
                z, sp, upto = _stick_scores(qs[t], kblk, tri, None, False)
                new_accs.append(accs[t] + _stick_values(z, upto, carries[t], vblk, None, False))
                new_carries.append(carries[t] + _row_sum(sp))
            return j - 1, tuple(new_accs), tuple(new_carries), live(new_carries)

        return lax.while_loop(cond, body, (qi - 2, tuple(accs), tuple(carries), live(carries)))[1]

    started = [newest_two(b) for b in range(n_q)]
    for b, state in enumerate(started):
        accs = older(*state)
        for t in range(n_tiles):
            o_ref[0, b * blk:(b + 1) * blk, t * LANES:(t + 1) * LANES] = jnp.where(
                lower, accs[t][:blk], accs[t][blk:]).astype(o_ref.dtype)


def _tri(blk):
    j = lax.broadcasted_iota(jnp.int32, (blk, blk), 0)
    s = lax.broadcasted_iota(jnp.int32, (blk, blk), 1)
    t = (j >= s).astype(BF16)
    return jnp.concatenate([t, t], axis=0)


def _attn_prompt(q, kb, vb, blk, q_blocks):
    b, t, _ = q.shape
    qspec = pl.BlockSpec((1, q_blocks * blk, ATTN_WIDTH), lambda bi, qi: (bi, qi, 0))
    kvspec = pl.BlockSpec((1, t, ATTN_WIDTH), lambda bi, qi: (bi, 0, 0))
    return pl.pallas_call(
        functools.partial(_attn_prompt_kernel, blk=blk),
        grid=(b, t // (q_blocks * blk)),
        in_specs=[qspec, kvspec, kvspec, _const_spec((2 * blk, blk))],
        out_specs=qspec,
        out_shape=jax.ShapeDtypeStruct((b, t, ATTN_WIDTH), BF16),
        compiler_params=pltpu.CompilerParams(
            dimension_semantics=("arbitrary", "arbitrary"),
            vmem_limit_bytes=VMEM_LIMIT),
        name="attn_prompt",
    )(q, kb, vb, _tri(blk))


def _attn_sample_kernel(q_ref, kn_ref, vn_ref, ck_ref, cv_ref, tri_ref, o_ref, *, blk, new_blk):
    tq = q_ref.shape[1]
    rows = N_HEADS * tq
    q = q_ref[0]
    qrep = jnp.concatenate([q] * N_HEADS, axis=0)
    r = lax.broadcasted_iota(jnp.int32, (rows, ATTN_WIDTH), 0)
    c = lax.broadcasted_iota(jnp.int32, (rows, ATTN_WIDTH), 1)
    tq_bits = tq.bit_length() - 1
    head_bits = HEAD_DIM.bit_length() - 1
    assert tq == 1 << tq_bits
    head_sel = (r >> tq_bits) == (c >> head_bits)
    qrows = jnp.where(head_sel, qrep, jnp.zeros_like(qrep))
    tri = tri_ref[...]

    pad = jnp.zeros((new_blk - tq, ATTN_WIDTH), BF16)
    kblk = jnp.concatenate([kn_ref[0], pad], axis=0)
    vblk = jnp.concatenate([vn_ref[0], pad], axis=0)
    rn = lax.broadcasted_iota(jnp.int32, (rows, new_blk), 0)
    cn = lax.broadcasted_iota(jnp.int32, (rows, new_blk), 1)
    new_mask = cn < (rn & (tq - 1))
    tri_new = jnp.concatenate([tri[:new_blk, :new_blk], tri[:new_blk, :new_blk]], axis=0)
    blocks = [_stick_scores(qrows, kblk, tri_new, new_mask, False) + (vblk, new_mask, False)]
    past = ck_ref.shape[2]
    for j in reversed(range(past // blk)):
        kblk = ck_ref[0, :, j * blk:(j + 1) * blk].astype(BF16)
        vblk = cv_ref[0, :, j * blk:(j + 1) * blk].astype(BF16)
        blocks.append(_stick_scores(qrows, kblk, tri, None, True) + (vblk, None, True))
    carry = jnp.zeros((rows, 1), F32)
    parts = []
    for z, sp, upto, vblk, mask, v_transposed in blocks:
        parts.append(_stick_values(z, upto, carry, vblk, mask, v_transposed))
        carry = carry + _row_sum(sp)
    while len(parts) > 1:
        parts = [functools.reduce(jnp.add, parts[i:i + 2]) for i in range(0, len(parts), 2)]
    acc = parts[0]

    co = lax.broadcasted_iota(jnp.int32, (tq, ATTN_WIDTH), 1) >> head_bits
    out = jnp.zeros((tq, ATTN_WIDTH), F32)
    for h in range(N_HEADS):
        out = jnp.where(co == h, acc[h * tq:(h + 1) * tq], out)
    o_ref[0] = out.astype(o_ref.dtype)


def _attn_sample(q, kb, vb, cache_k, cache_v, blk, new_blk):
    b, tq, _ = q.shape
    past = cache_k.shape[2]
    new = pl.BlockSpec((1, tq, ATTN_WIDTH), lambda bi: (bi, 0, 0))
    old = pl.BlockSpec((1, ATTN_WIDTH, past), lambda bi: (bi, 0, 0))
    return pl.pallas_call(
        functools.partial(_attn_sample_kernel, blk=blk, new_blk=new_blk),
        grid=(b,),
        in_specs=[new, new, new, old, old, _const_spec((2 * blk, blk))],
        out_specs=new,
        out_shape=jax.ShapeDtypeStruct((b, tq, ATTN_WIDTH), BF16),
        compiler_params=pltpu.CompilerParams(dimension_semantics=("arbitrary",),
                                             vmem_limit_bytes=VMEM_LIMIT),
        name="attn_sample",
    )(q, kb, vb, cache_k, cache_v, _tri(blk))


def _with_history(hist, cur):
    nb, h, c = hist.shape
    t = cur.shape[1]
    return jnp.concatenate([hist, cur], axis=1).reshape(nb * (h + t), c)


def _drop_history(flat, nb, h, t):
    c = flat.shape[-1]
    return flat.reshape(nb, h + t, c)[:, h:, :].reshape(nb * t, c)


def _twice_gelu_tanh(x):
    c = 0.7978845608028654
    return x * (1.0 + jnp.tanh(x * (c + (c * 0.044715) * (x * x))))


def _post_kernel(x_ref, attn_ref, u_ref, ga_ref, gb_ref, pleft_ref, cleft_ref,
                 wa_ref, wb_ref, pw_ref, ps_ref, wo_ref, nffn_ref, wup_ref, cw_ref, cb_ref,
                 wdn_ref, nfin_ref,
                 y_ref, ptail_ref, ctail_ref,
                 uhist_ref, chist_ref, h2_ref, act_ref, *, pos0):
    nb, tt, _ = x_ref.shape
    m = nb * tt
    ti = pl.program_id(1)

    @pl.when(ti == 0)
    def _():
        uhist_ref[...] = pleft_ref[...]
        chist_ref[...] = cleft_ref[...]

    branch_a = jnp.dot(attn_ref[...].reshape(m, ATTN_WIDTH), wa_ref[...], preferred_element_type=F32)
    x = x_ref[...].reshape(m, D_MODEL)
    u3 = u_ref[...]
    u = u3.reshape(m, POOL_WIDTH)

    ext = _with_history(uhist_ref[...], u3)
    row = lax.broadcasted_iota(jnp.int32, (m, POOL_GROUP), 0)
    pos = pos0 + ti * tt + (row & (tt - 1))
    pmix = []
    for g, w in enumerate(POOL_WINDOWS):
        sl = slice(g * POOL_GROUP, (g + 1) * POOL_GROUP)
        s = ext[:, sl]
        shift = 1
        while shift < w:
            s = s + pltpu.roll(s, shift, axis=0)
            shift *= 2
        win = _drop_history(s, nb, POOL_HALO, tt)
        cnt = jnp.minimum(pos + 1, w).astype(F32)
        pooled = win / cnt - u[:, sl]
        pmix.append(jnp.dot(pooled.astype(BF16), pw_ref[g], preferred_element_type=F32))
    pmix = jnp.concatenate(pmix, axis=1) * ps_ref[...]
    branch_b = jnp.dot(pmix.astype(BF16), wb_ref[...], preferred_element_type=F32)
    merged = (ga_ref[...].reshape(m, D_MODEL).astype(F32) * branch_a
              + gb_ref[...].reshape(m, D_MODEL).astype(F32) * branch_b)
    x1 = x + jnp.dot(merged.astype(BF16), wo_ref[...], preferred_element_type=F32)

    tail = u3[:, tt - POOL_HALO:, :]
    uhist_ref[...] = tail
    ptail_ref[...] = tail

    y_ref[...] = x1.reshape(nb, tt, D_MODEL)
    h2_ref[...] = _rmsnorm(x1, nffn_ref[...]).astype(BF16)
    n_chunks = D_FF // FF_CHUNK

    def up_proj(j):
        cols = [slice(half * D_FF + j * FF_CHUNK, half * D_FF + (j + 1) * FF_CHUNK) for half in range(2)]
        return [(cs, jnp.dot(h2_ref[...], wup_ref[:, cs], preferred_element_type=F32)) for cs in cols]

    ups = up_proj(0)
    for j in range(n_chunks):
        cur, ups = ups, (up_proj(j + 1) if j + 1 < n_chunks else None)
        halves = []
        for cs, up in cur:
            up3 = up.reshape(nb, tt, FF_CHUNK)
            e = _with_history(chist_ref[:, :, cs], up3)
            conv = cb_ref[:, cs] + up * cw_ref[2:3, cs]
            for d in (1, 2):
                shifted = _drop_history(pltpu.roll(e, d, axis=0), nb, CONV_HALO, tt)
                conv = conv + shifted * cw_ref[2 - d:3 - d, cs]
            halves.append(conv)
            ctile = up3[:, tt - CONV_HALO:, :]
            chist_ref[:, :, cs] = ctile
            ctail_ref[:, :, cs] = ctile
        act = (_twice_gelu_tanh(halves[0]) * halves[1]).astype(BF16)
        act_ref[:, j * FF_CHUNK:(j + 1) * FF_CHUNK] = act
    down = jnp.dot(act_ref[...], wdn_ref[...], preferred_element_type=F32)
    y = _rmsnorm(y_ref[...].reshape(m, D_MODEL) + down, nfin_ref[...])
    y_ref[...] = y.reshape(nb, tt, D_MODEL)


def _post(x, attn, u, ga, gb, pool_left, conv_left, weights, nb, tt, pos0):
    b, t, _ = x.shape
    (w_a, w_b, pool_w, pool_scale, w_o, norm_ffn, w_up, conv_w, conv_b, w_down, norm_final) = weights
    tile = lambda w: pl.BlockSpec((nb, tt, w), lambda bi, ti: (bi, ti, 0))
    per_seq = lambda r, w: pl.BlockSpec((nb, r, w), lambda bi, ti: (bi, 0, 0))
    consts = [w_a, w_b, pool_w, pool_scale, w_o, norm_ffn, w_up, conv_w, conv_b, w_down, norm_final]
    return pl.pallas_call(
        functools.partial(_post_kernel, pos0=pos0),
        grid=(b // nb, t // tt),
        in_specs=[tile(D_MODEL), tile(ATTN_WIDTH), tile(POOL_WIDTH), tile(D_MODEL), tile(D_MODEL),
                  per_seq(POOL_HALO, POOL_WIDTH), per_seq(CONV_HALO, 2 * D_FF)]
                 + [_const_spec(c.shape) for c in consts],
        out_specs=[tile(D_MODEL), per_seq(POOL_HALO, POOL_WIDTH), per_seq(CONV_HALO, 2 * D_FF)],
        out_shape=[jax.ShapeDtypeStruct((b, t, D_MODEL), F32),
                   jax.ShapeDtypeStruct((b, POOL_HALO, POOL_WIDTH), F32),
                   jax.ShapeDtypeStruct((b, CONV_HALO, 2 * D_FF), F32)],
        scratch_shapes=[pltpu.VMEM((nb, POOL_HALO, POOL_WIDTH), F32),
                        pltpu.VMEM((nb, CONV_HALO, 2 * D_FF), F32),
                        pltpu.VMEM((nb * tt, D_MODEL), BF16),
                        pltpu.VMEM((nb * tt, D_FF), BF16)],
        compiler_params=pltpu.CompilerParams(dimension_semantics=("arbitrary", "arbitrary"),
                                             vmem_limit_bytes=VMEM_LIMIT),
        name="post",
    )(x, attn, u, ga, gb, pool_left, conv_left, *consts)


def _left_pad(state, rows):
    b, r, c = state.shape
    return jnp.concatenate([jnp.zeros((b, rows - r, c), state.dtype), state], axis=1)


def _stream(x, past, pool_left, conv_left, norm_mix, w_in_bf, post_weights, *, tm, nb, tt, pos0):
    b, t, _ = x.shape
    m = b * t
    kv_transposed = past is None
    q, k, v, u, kb, vb, ga, gb = _inproj(x.reshape(m, D_MODEL), norm_mix, w_in_bf, tm, t, kv_transposed)
    seq = lambda a: a.reshape(b, t, a.shape[-1])
    if past is None:
        attn = _attn_prompt(seq(q), seq(kb), seq(vb), blk=256, q_blocks=2)
        heads = lambda a: a.reshape(1, b, N_HEADS, HEAD_DIM, t).transpose(0, 1, 4, 2, 3)
    else:
        time_last = lambda c: c.transpose(0, 2, 3, 1).reshape(b, ATTN_WIDTH, c.shape[1])
        attn = _attn_sample(seq(q), seq(kb), seq(vb), time_last(past[0]), time_last(past[1]),
                            blk=256, new_blk=128)
        heads = lambda a: a.reshape(1, b, t, N_HEADS, HEAD_DIM)
    y, ptail, ctail = _post(x, attn, seq(u), seq(ga), seq(gb),
                            _left_pad(pool_left, POOL_HALO), _left_pad(conv_left, CONV_HALO),
                            post_weights, nb, tt, pos0)
    return (y, heads(k), heads(v),
            ptail[None, :, POOL_HALO - POOL_STATE:, :],
            ctail[None, :, CONV_HALO - (CONV_WIDTH - 1):, :])


def kernel(x_prompt, x_sample, cache_k, cache_v, state_pool, state_conv, norm_mix, w_in, w_a, w_b,
           pool_w, pool_scale, w_o, norm_ffn, w_up, conv_w, conv_b, w_down, norm_final):
    assert w_in.shape[0] == 1, "single-layer trunk"
    row = lambda a: a.reshape(1, -1)
    post_weights = (w_a[0].astype(BF16), w_b[0].astype(BF16), pool_w[0].astype(BF16),
                    row(pool_scale[0]), w_o[0].astype(BF16), row(norm_ffn[0]),
                    w_up[0].astype(BF16), conv_w[0], row(conv_b[0]), (0.5 * w_down[0]).astype(BF16),
                    row(norm_final))
    w_in_bf = w_in[0].astype(BF16)
    bp = x_prompt.shape[0]
    zeros_pool = jnp.zeros((bp, POOL_STATE, POOL_WIDTH), F32)
    zeros_conv = jnp.zeros((bp, CONV_WIDTH - 1, 2 * D_FF), F32)
    yp, kp, vp, pp, cp = _stream(x_prompt, None, zeros_pool, zeros_conv, norm_mix[0], w_in_bf,
                                 post_weights, tm=512, nb=1, tt=512, pos0=0)
    bs, ts, _ = x_sample.shape
    ys, ks, vs, ps, cs = _stream(x_sample, (cache_k[0], cache_v[0]), state_pool[0], state_conv[0],
                                 norm_mix[0], w_in_bf, post_weights,
                                 tm=bs * ts, nb=bs, tt=ts, pos0=cache_k.shape[2])
    return (yp, ys, kp, vp, pp, cp, ks, vs, ps, cs)
```

```python
import functools

import jax
import jax.numpy as jnp
from jax import lax
from jax.experimental import pallas as pl
from jax.experimental.pallas import tpu as pltpu

D_MODEL = 1024
N_HEADS = 8
HEAD_DIM = 64
ATTN_WIDTH = N_HEADS * HEAD_DIM
POOL_WINDOWS = (2, 4, 8, 16)
POOL_WIDTH = 512
POOL_GROUP = POOL_WIDTH // len(POOL_WINDOWS)
POOL_STATE = max(POOL_WINDOWS) - 1
D_FF = 2816
CONV_WIDTH = 3
EPS = 1e-6
IN_WIDTH = 3 * ATTN_WIDTH + POOL_WIDTH + 2 * D_MODEL

LANES = 128
HEADS_PER_LANE_TILE = LANES // HEAD_DIM
POOL_HALO = 16
CONV_HALO = 8
FF_CHUNK = 256
VMEM_LIMIT = 56 * 1024 * 1024
LOG2E = 1.4426950408889634
DEAD_LOG2 = 160.0
SOFTPLUS_CLAMP = 126.0

F32 = jnp.float32
BF16 = jnp.bfloat16


def _rmsnorm(x, g):
    return x * lax.rsqrt(jnp.mean(x * x, axis=-1, keepdims=True) + EPS) * g


def _const_spec(shape):
    nd = len(shape)
    return pl.BlockSpec(shape, lambda *_: (0,) * nd, pipeline_mode=pl.Buffered(1))


def _inproj_kernel(x_ref, g_ref, w_ref, q_ref, k_ref, v_ref, u_ref, kb_ref, vb_ref, ga_ref, gb_ref,
                   *, kv_transposed):
    h = _rmsnorm(x_ref[...], g_ref[...]).astype(BF16)

    def proj(lo, hi):
        return jnp.dot(h, w_ref[:, lo:hi], preferred_element_type=F32)

    a = ATTN_WIDTH
    u0 = 3 * a
    g0 = u0 + POOL_WIDTH
    ga_ref[...] = jax.nn.sigmoid(proj(g0, g0 + D_MODEL)).astype(BF16)
    gb_ref[...] = jax.nn.sigmoid(proj(g0 + D_MODEL, g0 + 2 * D_MODEL)).astype(BF16)
    k = proj(a, 2 * a)
    kb_ref[...] = k.astype(BF16)
    v = proj(2 * a, 3 * a)
    vb_ref[...] = v.astype(BF16)
    if kv_transposed:
        k_ref[0] = k.T
        v_ref[0] = v.T
    else:
        k_ref[...] = k
        v_ref[...] = v
    q_ref[...] = (proj(0, a) * (HEAD_DIM ** -0.5 * LOG2E)).astype(BF16)
    u_ref[...] = proj(u0, u0 + POOL_WIDTH)


def _inproj(x2d, norm_mix, w_in_bf, tm, seq_len, kv_transposed):
    m = x2d.shape[0]
    row = lambda w: pl.BlockSpec((tm, w), lambda i: (i, 0))
    outs = [(ATTN_WIDTH, BF16), (ATTN_WIDTH, F32), (ATTN_WIDTH, F32), (POOL_WIDTH, F32),
            (ATTN_WIDTH, BF16), (ATTN_WIDTH, BF16), (D_MODEL, BF16), (D_MODEL, BF16)]
    out_specs = [row(w) for w, _ in outs]
    out_shape = [jax.ShapeDtypeStruct((m, w), dt) for w, dt in outs]
    if kv_transposed:
        tiles = seq_len // tm
        for i in (1, 2):
            out_specs[i] = pl.BlockSpec((1, ATTN_WIDTH, tm), lambda i: (i // tiles, 0, i % tiles))
            out_shape[i] = jax.ShapeDtypeStruct((m // seq_len, ATTN_WIDTH, seq_len), F32)
    return pl.pallas_call(
        functools.partial(_inproj_kernel, kv_transposed=kv_transposed),
        grid=(m // tm,),
        in_specs=[row(D_MODEL), _const_spec((1, D_MODEL)), _const_spec((D_MODEL, IN_WIDTH))],
        out_specs=out_specs,
        out_shape=out_shape,
        compiler_params=pltpu.CompilerParams(dimension_semantics=("arbitrary",),
                                             vmem_limit_bytes=VMEM_LIMIT),
        name="inproj",
    )(x2d, norm_mix.reshape(1, D_MODEL), w_in_bf)


_NT = (((1,), (1,)), ((), ()))


def _softplus2(z):
    return jnp.maximum(jnp.log2(1.0 + jnp.exp2(jnp.minimum(z, SOFTPLUS_CLAMP))), z)


def _stick_scores(qrows, kblk, tri, mask, k_transposed):
    if k_transposed:
        z = jnp.dot(qrows, kblk, preferred_element_type=F32)
    else:
        z = lax.dot_general(qrows, kblk, _NT, preferred_element_type=F32)
    sp = _softplus2(z)
    if mask is not None:
        sp = jnp.where(mask, sp, 0.0)
    hi = sp.astype(BF16)
    lo = (sp - hi.astype(F32)).astype(BF16)
    upto = jnp.dot(jnp.concatenate([hi, lo], axis=1), tri, preferred_element_type=F32)
    return z, sp, upto


def _stick_values(z, upto, carry, vblk, mask, v_transposed):
    a = jnp.exp2(z - upto - carry)
    if mask is not None:
        a = jnp.where(mask, a, 0.0)
    a = a.astype(BF16)
    if v_transposed:
        return lax.dot_general(a, vblk, _NT, preferred_element_type=F32)
    return jnp.dot(a, vblk, preferred_element_type=F32)


def _row_sum(sp):
    return jnp.sum(sp, axis=1, keepdims=True)


def _attn_prompt_kernel(q_ref, k_ref, v_ref, tri_ref, o_ref, *, blk):
    n_q = q_ref.shape[1] // blk
    n_tiles = q_ref.shape[2] // LANES
    rows = 2 * blk
    tri = tri_ref[...]
    lower = lax.broadcasted_iota(jnp.int32, (blk, LANES), 1) < HEAD_DIM
    r = lax.broadcasted_iota(jnp.int32, (rows, blk), 0)
    c = lax.broadcasted_iota(jnp.int32, (rows, blk), 1)
    diag_mask = c < (r & (blk - 1))
    no_carry = jnp.zeros((rows, 1), F32)

    def q_rows(b, t):
        q2 = q_ref[0, b * blk:(b + 1) * blk, t * LANES:(t + 1) * LANES]
        zero = jnp.zeros_like(q2)
        return jnp.concatenate([jnp.where(lower, q2, zero), jnp.where(lower, zero, q2)], axis=0)

    def kv(j, t):
        start = pl.multiple_of(j * blk, blk)
        lanes = slice(t * LANES, (t + 1) * LANES)
        return k_ref[0, pl.ds(start, blk), lanes], v_ref[0, pl.ds(start, blk), lanes]

    def newest_two(b):
        qi = pl.program_id(1) * n_q + b
        has_prev = qi > 0
        qs, accs, carries = [], [], []
        for t in range(n_tiles):
            q = q_rows(b, t)
            kd, vd = kv(qi, t)
            kp, vp = kv(jnp.maximum(qi - 1, 0), t)
            zd, spd, upd = _stick_scores(q, kd, tri, diag_mask, False)
            zp, spp, upp = _stick_scores(q, kp, tri, None, False)
            carry_d = _row_sum(spd)
            pv_d = _stick_values(zd, upd, no_carry, vd, diag_mask, False)
            pv_p = _stick_values(zp, upp, carry_d, vp, None, False)
            qs.append(q)
            accs.append(pv_d + jnp.where(has_prev, pv_p, 0.0))
            carries.append(carry_d + jnp.where(has_prev, _row_sum(spp), 0.0))
        return qi, qs, accs, carries

    def live(carries):
        return jnp.min(functools.reduce(jnp.minimum, carries)) < DEAD_LOG2

    def older(qi, qs, accs, carries):
        def cond(state):
            j, _, _, alive = state
            return jnp.logical_and(j >= 0, alive)

        def body(state):
            j, accs, carries, _ = state
            new_accs, new_carries = [], []
            for t in range(n_tiles):
                kblk, vblk = kv(j, t)
                z, sp, upto = _stick_scores(qs[t], kblk, tri, None, False)
                new_accs.append(accs[t] + _stick_values(z, upto, carries[t], vblk, None, False))
                new_carries.append(carries[t] + _row_sum(sp))
            return j - 1, tuple(new_accs), tuple(new_carries), live(new_carries)

        return lax.while_loop(cond, body, (qi - 2, tuple(accs), tuple(carries), live(carries)))[1]

    started = [newest_two(b) for b in range(n_q)]
    for b, state in enumerate(started):
        accs = older(*state)
        for t in range(n_tiles):
            o_ref[0, b * blk:(b + 1) * blk, t * LANES:(t + 1) * LANES] = jnp.where(
                lower, accs[t][:blk], accs[t][blk:]).astype(o_ref.dtype)


def _tri(blk):
    j = lax.broadcasted_iota(jnp.int32, (blk, blk), 0)
    s = lax.broadcasted_iota(jnp.int32, (blk, blk), 1)
    t = (j >= s).astype(BF16)
    return jnp.concatenate([t, t], axis=0)


def _attn_prompt(q, kb, vb, blk, q_blocks):
    b, t, _ = q.shape
    qspec = pl.BlockSpec((1, q_blocks * blk, ATTN_WIDTH), lambda bi, qi: (bi, qi, 0))
    kvspec = pl.BlockSpec((1, t, ATTN_WIDTH), lambda bi, qi: (bi, 0, 0))
    return pl.pallas_call(
        functools.partial(_attn_prompt_kernel, blk=blk),
        grid=(b, t // (q_blocks * blk)),
        in_specs=[qspec, kvspec, kvspec, _const_spec((2 * blk, blk))],
        out_specs=qspec,
        out_shape=jax.ShapeDtypeStruct((b, t, ATTN_WIDTH), BF16),
        compiler_params=pltpu.CompilerParams(
            dimension_semantics=("arbitrary", "arbitrary"),
            vmem_limit_bytes=VMEM_LIMIT),
        name="attn_prompt",
    )(q, kb, vb, _tri(blk))


def _attn_sample_kernel(q_ref, kn_ref, vn_ref, ck_ref, cv_ref, tri_ref, o_ref, carry_ref, *, blk, new_blk):
    tq = q_ref.shape[1]
    rows = N_HEADS * tq
    q = q_ref[0]
    qrep = jnp.concatenate([q] * N_HEADS, axis=0)
    r = lax.broadcasted_iota(jnp.int32, (rows, ATTN_WIDTH), 0)
    c = lax.broadcasted_iota(jnp.int32, (rows, ATTN_WIDTH), 1)
    tq_bits = tq.bit_length() - 1
    head_bits = HEAD_DIM.bit_length() - 1
    assert tq == 1 << tq_bits
    head_sel = (r >> tq_bits) == (c >> head_bits)
    qrows = jnp.where(head_sel, qrep, jnp.zeros_like(qrep))
    tri = tri_ref[...]

    pad = jnp.zeros((new_blk - tq, ATTN_WIDTH), BF16)
    kblk = jnp.concatenate([kn_ref[0], pad], axis=0)
    vblk = jnp.concatenate([vn_ref[0], pad], axis=0)
    rn = lax.broadcasted_iota(jnp.int32, (rows, new_blk), 0)
    cn = lax.broadcasted_iota(jnp.int32, (rows, new_blk), 1)
    new_mask = cn < (rn & (tq - 1))
    tri_new = jnp.concatenate([tri[:new_blk, :new_blk], tri[:new_blk, :new_blk]], axis=0)
    blocks = [_stick_scores(qrows, kblk, tri_new, new_mask, False) + (vblk, new_mask, False)]
    window = ck_ref.shape[2]
    for j in reversed(range(window // blk)):
        kblk = ck_ref[0, :, j * blk:(j + 1) * blk].astype(BF16)
        vblk = cv_ref[0, :, j * blk:(j + 1) * blk].astype(BF16)
        blocks.append(_stick_scores(qrows, kblk, tri, None, True) + (vblk, None, True))
    carry = jnp.zeros((rows, 1), F32)
    parts = []
    for z, sp, upto, vblk, mask, v_transposed in blocks:
        parts.append(_stick_values(z, upto, carry, vblk, mask, v_transposed))
        carry = carry + _row_sum(sp)
    while len(parts) > 1:
        parts = [functools.reduce(jnp.add, parts[i:i + 2]) for i in range(0, len(parts), 2)]
    acc = parts[0]

    co = lax.broadcasted_iota(jnp.int32, (tq, ATTN_WIDTH), 1) >> head_bits
    out = jnp.zeros((tq, ATTN_WIDTH), F32)
    for h in range(N_HEADS):
        out = jnp.where(co == h, acc[h * tq:(h + 1) * tq], out)
    o_ref[0] = out.astype(o_ref.dtype)
    carry_ref[0] = jnp.broadcast_to(jnp.min(carry, axis=0, keepdims=True), carry_ref.shape[1:])


def _attn_sample_window(q, kb, vb, cache_k, cache_v, blk, new_blk, window):
    b, tq, _ = q.shape
    past = cache_k.shape[2]
    new = pl.BlockSpec((1, tq, ATTN_WIDTH), lambda bi: (bi, 0, 0))
    old = pl.BlockSpec((1, ATTN_WIDTH, window), lambda bi: (bi, 0, past // window - 1))
    low = pl.BlockSpec((1, 8, LANES), lambda bi: (bi, 0, 0))
    return pl.pallas_call(
        functools.partial(_attn_sample_kernel, blk=blk, new_blk=new_blk),
        grid=(b,),
        in_specs=[new, new, new, old, old, _const_spec((2 * blk, blk))],
        out_specs=[new, low],
        out_shape=[jax.ShapeDtypeStruct((b, tq, ATTN_WIDTH), BF16),
                   jax.ShapeDtypeStruct((b, 8, LANES), F32)],
        compiler_params=pltpu.CompilerParams(dimension_semantics=("arbitrary",),
                                             vmem_limit_bytes=VMEM_LIMIT),
        name="attn_sample",
    )(q, kb, vb, cache_k, cache_v, _tri(blk))


def _attn_sample(q, kb, vb, cache_k, cache_v, blk, new_blk):
    past = cache_k.shape[2]
    recent, min_carry = _attn_sample_window(q, kb, vb, cache_k, cache_v, blk, new_blk, blk)
    if past == blk:
        return recent
    return lax.cond(jnp.min(min_carry) < DEAD_LOG2,
                    lambda: _attn_sample_window(q, kb, vb, cache_k, cache_v, blk, new_blk, past)[0],
                    lambda: recent)


def _with_history(hist, cur):
    nb, h, c = hist.shape
    t = cur.shape[1]
    return jnp.concatenate([hist, cur], axis=1).reshape(nb * (h + t), c)


def _drop_history(flat, nb, h, t):
    c = flat.shape[-1]
    return flat.reshape(nb, h + t, c)[:, h:, :].reshape(nb * t, c)


def _twice_gelu_tanh(x):
    c = 0.7978845608028654
    return x * (1.0 + jnp.tanh(x * (c + (c * 0.044715) * (x * x))))


def _post_kernel(x_ref, attn_ref, u_ref, ga_ref, gb_ref, pleft_ref, cleft_ref,
                 wa_ref, wb_ref, pw_ref, ps_ref, wo_ref, nffn_ref, wup_ref, cw_ref, cb_ref,
                 wdn_ref, nfin_ref,
                 y_ref, ptail_ref, ctail_ref,
                 uhist_ref, chist_ref, h2_ref, act_ref, *, pos0):
    nb, tt, _ = x_ref.shape
    m = nb * tt
    ti = pl.program_id(1)

    @pl.when(ti == 0)
    def _():
        uhist_ref[...] = pleft_ref[...]
        chist_ref[...] = cleft_ref[...]

    branch_a = jnp.dot(attn_ref[...].reshape(m, ATTN_WIDTH), wa_ref[...], preferred_element_type=F32)
    x = x_ref[...].reshape(m, D_MODEL)
    u3 = u_ref[...]
    u = u3.reshape(m, POOL_WIDTH)

    ext = _with_history(uhist_ref[...], u3)
    row = lax.broadcasted_iota(jnp.int32, (m, POOL_GROUP), 0)
    pos = pos0 + ti * tt + (row & (tt - 1))
    pmix = []
    for g, w in enumerate(POOL_WINDOWS):
        sl = slice(g * POOL_GROUP, (g + 1) * POOL_GROUP)
        s = ext[:, sl]
        shift = 1
        while shift < w:
            s = s + pltpu.roll(s, shift, axis=0)
            shift *= 2
        win = _drop_history(s, nb, POOL_HALO, tt)
        cnt = jnp.minimum(pos + 1, w).astype(F32)
        pooled = win / cnt - u[:, sl]
        pmix.append(jnp.dot(pooled.astype(BF16), pw_ref[g], preferred_element_type=F32))
    pmix = jnp.concatenate(pmix, axis=1) * ps_ref[...]
    branch_b = jnp.dot(pmix.astype(BF16), wb_ref[...], preferred_element_type=F32)
    merged = (ga_ref[...].reshape(m, D_MODEL).astype(F32) * branch_a
              + gb_ref[...].reshape(m, D_MODEL).astype(F32) * branch_b)
    x1 = x + jnp.dot(merged.astype(BF16), wo_ref[...], preferred_element_type=F32)

    tail = u3[:, tt - POOL_HALO:, :]
    uhist_ref[...] = tail
    ptail_ref[...] = tail[:, POOL_HALO - POOL_STATE:, :]

    y_ref[...] = x1.reshape(nb, tt, D_MODEL)
    h2_ref[...] = _rmsnorm(x1, nffn_ref[...]).astype(BF16)
    n_chunks = D_FF // FF_CHUNK

    def up_proj(j):
        cols = [slice(half * D_FF + j * FF_CHUNK, half * D_FF + (j + 1) * FF_CHUNK) for half in range(2)]
        return [(cs, jnp.dot(h2_ref[...], wup_ref[:, cs], preferred_element_type=F32)) for cs in cols]

    ups = up_proj(0)
    for j in range(n_chunks):
        cur, ups = ups, (up_proj(j + 1) if j + 1 < n_chunks else None)
        halves = []
        for cs, up in cur:
            up3 = up.reshape(nb, tt, FF_CHUNK)
            e = _with_history(chist_ref[:, :, cs], up3)
            conv = cb_ref[:, cs] + up * cw_ref[2:3, cs]
            for d in (1, 2):
                shifted = _drop_history(pltpu.roll(e, d, axis=0), nb, CONV_HALO, tt)
                conv = conv + shifted * cw_ref[2 - d:3 - d, cs]
            halves.append(conv)
            ctile = up3[:, tt - CONV_HALO:, :]
            chist_ref[:, :, cs] = ctile
            ctail_ref[:, :, cs] = ctile[:, CONV_HALO - (CONV_WIDTH - 1):, :]
        act = (_twice_gelu_tanh(halves[0]) * halves[1]).astype(BF16)
        act_ref[:, j * FF_CHUNK:(j + 1) * FF_CHUNK] = act
    down = jnp.dot(act_ref[...], wdn_ref[...], preferred_element_type=F32)
    y = _rmsnorm(y_ref[...].reshape(m, D_MODEL) + down, nfin_ref[...])
    y_ref[...] = y.reshape(nb, tt, D_MODEL)


def _post(x, attn, u, ga, gb, pool_left, conv_left, weights, nb, tt, pos0):
    b, t, _ = x.shape
    (w_a, w_b, pool_w, pool_scale, w_o, norm_ffn, w_up, conv_w, conv_b, w_down, norm_final) = weights
    tile = lambda w: pl.BlockSpec((nb, tt, w), lambda bi, ti: (bi, ti, 0))
    per_seq = lambda r, w: pl.BlockSpec((nb, r, w), lambda bi, ti: (bi, 0, 0))
    consts = [w_a, w_b, pool_w, pool_scale, w_o, norm_ffn, w_up, conv_w, conv_b, w_down, norm_final]
    return pl.pallas_call(
        functools.partial(_post_kernel, pos0=pos0),
        grid=(b // nb, t // tt),
        in_specs=[tile(D_MODEL), tile(ATTN_WIDTH), tile(POOL_WIDTH), tile(D_MODEL), tile(D_MODEL),
                  per_seq(POOL_HALO, POOL_WIDTH), per_seq(CONV_HALO, 2 * D_FF)]
                 + [_const_spec(c.shape) for c in consts],
        out_specs=[tile(D_MODEL), per_seq(POOL_STATE, POOL_WIDTH), per_seq(CONV_WIDTH - 1, 2 * D_FF)],
        out_shape=[jax.ShapeDtypeStruct((b, t, D_MODEL), F32),
                   jax.ShapeDtypeStruct((b, POOL_STATE, POOL_WIDTH), F32),
                   jax.ShapeDtypeStruct((b, CONV_WIDTH - 1, 2 * D_FF), F32)],
        scratch_shapes=[pltpu.VMEM((nb, POOL_HALO, POOL_WIDTH), F32),
                        pltpu.VMEM((nb, CONV_HALO, 2 * D_FF), F32),
                        pltpu.VMEM((nb * tt, D_MODEL), BF16),
                        pltpu.VMEM((nb * tt, D_FF), BF16)],
        compiler_params=pltpu.CompilerParams(dimension_semantics=("arbitrary", "arbitrary"),
                                             vmem_limit_bytes=VMEM_LIMIT),
        name="post",
    )(x, attn, u, ga, gb, pool_left, conv_left, *consts)


def _left_pad(state, rows):
    b, r, c = state.shape
    return jnp.concatenate([jnp.zeros((b, rows - r, c), state.dtype), state], axis=1)


def _stream(x, past, pool_left, conv_left, norm_mix, w_in_bf, post_weights, *, tm, nb, tt, pos0):
    b, t, _ = x.shape
    m = b * t
    kv_transposed = past is None
    q, k, v, u, kb, vb, ga, gb = _inproj(x.reshape(m, D_MODEL), norm_mix, w_in_bf, tm, t, kv_transposed)
    seq = lambda a: a.reshape(b, t, a.shape[-1])
    if past is None:
        attn = _attn_prompt(seq(q), seq(kb), seq(vb), blk=256, q_blocks=2)
        heads = lambda a: a.reshape(1, b, N_HEADS, HEAD_DIM, t).transpose(0, 1, 4, 2, 3)
    else:
        time_last = lambda c: c.transpose(0, 2, 3, 1).reshape(b, ATTN_WIDTH, c.shape[1])
        attn = _attn_sample(seq(q), seq(kb), seq(vb), time_last(past[0]), time_last(past[1]),
                            blk=256, new_blk=128)
        heads = lambda a: a.reshape(1, b, t, N_HEADS, HEAD_DIM)
    y, ptail, ctail = _post(x, attn, seq(u), seq(ga), seq(gb),
                            _left_pad(pool_left, POOL_HALO), _left_pad(conv_left, CONV_HALO),
                            post_weights, nb, tt, pos0)
    return (y, heads(k), heads(v),
            ptail[None], ctail[None])


def kernel(x_prompt, x_sample, cache_k, cache_v, state_pool, state_conv, norm_mix, w_in, w_a, w_b,
           pool_w, pool_scale, w_o, norm_ffn, w_up, conv_w, conv_b, w_down, norm_final):
    assert w_in.shape[0] == 1, "single-layer trunk"
    row = lambda a: a.reshape(1, -1)
    post_weights = (w_a[0].astype(BF16), w_b[0].astype(BF16), pool_w[0].astype(BF16),
                    row(pool_scale[0]), w_o[0].astype(BF16), row(norm_ffn[0]),
                    w_up[0].astype(BF16), conv_w[0], row(conv_b[0]), (0.5 * w_down[0]).astype(BF16),
                    row(norm_final))
    w_in_bf = w_in[0].astype(BF16)
    bp = x_prompt.shape[0]
    zeros_pool = jnp.zeros((bp, POOL_STATE, POOL_WIDTH), F32)
    zeros_conv = jnp.zeros((bp, CONV_WIDTH - 1, 2 * D_FF), F32)
    yp, kp, vp, pp, cp = _stream(x_prompt, None, zeros_pool, zeros_conv, norm_mix[0], w_in_bf,
                                 post_weights, tm=512, nb=1, tt=512, pos0=0)
    bs, ts, _ = x_sample.shape
    ys, ks, vs, ps, cs = _stream(x_sample, (cache_k[0], cache_v[0]), state_pool[0], state_conv[0],
                                 norm_mix[0], w_in_bf, post_weights,
                                 tm=bs * ts, nb=bs, tt=ts, pos0=cache_k.shape[2])
    return (yp, ys, kp, vp, pp, cp, ks, vs, ps, cs)
```

```python
import functools

import jax
import jax.numpy as jnp
from jax import lax
from jax.experimental import pallas as pl
from jax.experimental.pallas import tpu as pltpu

D_MODEL = 1024
N_HEADS = 8
HEAD_DIM = 64
ATTN_WIDTH = N_HEADS * HEAD_DIM
POOL_WINDOWS = (2, 4, 8, 16)
POOL_WIDTH = 512
POOL_GROUP = POOL_WIDTH // len(POOL_WINDOWS)
POOL_STATE = max(POOL_WINDOWS) - 1
D_FF = 2816
CONV_WIDTH = 3
EPS = 1e-6
IN_WIDTH = 3 * ATTN_WIDTH + POOL_WIDTH + 2 * D_MODEL

LANES = 128
HEADS_PER_LANE_TILE = LANES // HEAD_DIM
POOL_HALO = 16
CONV_HALO = 8
FF_CHUNK = 256
VMEM_LIMIT = 56 * 1024 * 1024
LOG2E = 1.4426950408889634
DEAD_LOG2 = 160.0
SOFTPLUS_CLAMP = 126.0

F32 = jnp.float32
BF16 = jnp.bfloat16


def _rmsnorm(x, g):
    return x * lax.rsqrt(jnp.mean(x * x, axis=-1, keepdims=True) + EPS) * g


def _const_spec(shape):
    nd = len(shape)
    return pl.BlockSpec(shape, lambda *_: (0,) * nd, pipeline_mode=pl.Buffered(1))


def _inproj_kernel(x_ref, g_ref, w_ref, q_ref, k_ref, v_ref, u_ref, kb_ref, vb_ref, ga_ref, gb_ref,
                   *, kv_transposed):
    h = _rmsnorm(x_ref[...], g_ref[...]).astype(BF16)

    def proj(lo, hi):
        return jnp.dot(h, w_ref[:, lo:hi], preferred_element_type=F32)

    a = ATTN_WIDTH
    u0 = 3 * a
    g0 = u0 + POOL_WIDTH
    ga_ref[...] = jax.nn.sigmoid(proj(g0, g0 + D_MODEL)).astype(BF16)
    gb_ref[...] = jax.nn.sigmoid(proj(g0 + D_MODEL, g0 + 2 * D_MODEL)).astype(BF16)
    k = proj(a, 2 * a)
    kb_ref[...] = k.astype(BF16)
    v = proj(2 * a, 3 * a)
    vb_ref[...] = v.astype(BF16)
    if kv_transposed:
        k_ref[0] = k.T
        v_ref[0] = v.T
    else:
        k_ref[...] = k
        v_ref[...] = v
    q_ref[...] = (proj(0, a) * (HEAD_DIM ** -0.5 * LOG2E)).astype(BF16)
    u_ref[...] = proj(u0, u0 + POOL_WIDTH)


def _inproj(x2d, norm_mix, w_in_bf, tm, seq_len, kv_transposed):
    m = x2d.shape[0]
    row = lambda w: pl.BlockSpec((tm, w), lambda i: (i, 0))
    outs = [(ATTN_WIDTH, BF16), (ATTN_WIDTH, F32), (ATTN_WIDTH, F32), (POOL_WIDTH, F32),
            (ATTN_WIDTH, BF16), (ATTN_WIDTH, BF16), (D_MODEL, BF16), (D_MODEL, BF16)]
    out_specs = [row(w) for w, _ in outs]
    out_shape = [jax.ShapeDtypeStruct((m, w), dt) for w, dt in outs]
    if kv_transposed:
        tiles = seq_len // tm
        for i in (1, 2):
            out_specs[i] = pl.BlockSpec((1, ATTN_WIDTH, tm), lambda i: (i // tiles, 0, i % tiles))
            out_shape[i] = jax.ShapeDtypeStruct((m // seq_len, ATTN_WIDTH, seq_len), F32)
    return pl.pallas_call(
        functools.partial(_inproj_kernel, kv_transposed=kv_transposed),
        grid=(m // tm,),
        in_specs=[row(D_MODEL), _const_spec((1, D_MODEL)), _const_spec((D_MODEL, IN_WIDTH))],
        out_specs=out_specs,
        out_shape=out_shape,
        compiler_params=pltpu.CompilerParams(dimension_semantics=("arbitrary",),
                                             vmem_limit_bytes=VMEM_LIMIT),
        name="inproj",
    )(x2d, norm_mix.reshape(1, D_MODEL), w_in_bf)


_NT = (((1,), (1,)), ((), ()))


def _softplus2(z):
    return jnp.maximum(jnp.log2(1.0 + jnp.exp2(jnp.minimum(z, SOFTPLUS_CLAMP))), z)


def _scores(qrows, kblk, k_transposed):
    if k_transposed:
        return jnp.dot(qrows, kblk, preferred_element_type=F32)
    return lax.dot_general(qrows, kblk, _NT, preferred_element_type=F32)


def _fail_bits(z, mask):
    sp = _softplus2(z)
    return sp if mask is None else jnp.where(mask, sp, 0.0)


def _upto(sp, tri):
    return jnp.dot(sp.astype(BF16), tri, preferred_element_type=F32)


def _weights(z, upto, carry, mask):
    a = jnp.exp2(z - upto - carry)
    if mask is not None:
        a = jnp.where(mask, a, 0.0)
    return a.astype(BF16)


def _apply(a, vblk, v_transposed):
    if v_transposed:
        return lax.dot_general(a, vblk, _NT, preferred_element_type=F32)
    return jnp.dot(a, vblk, preferred_element_type=F32)


def _stick_scores(qrows, kblk, tri, mask, k_transposed):
    z = _scores(qrows, kblk, k_transposed)
    sp = _fail_bits(z, mask)
    return z, sp, _upto(sp, tri)


def _stick_values(z, upto, carry, vblk, mask, v_transposed):
    return _apply(_weights(z, upto, carry, mask), vblk, v_transposed)


def _row_sum(sp):
    return jnp.sum(sp, axis=1, keepdims=True)


def _attn_prompt_kernel(q_ref, k_ref, v_ref, tri_ref, o_ref, *, blk):
    n_q = q_ref.shape[1] // blk
    n_tiles = q_ref.shape[2] // LANES
    rows = 2 * blk
    tri = tri_ref[...]
    lower = lax.broadcasted_iota(jnp.int32, (blk, LANES), 1) < HEAD_DIM
    r = lax.broadcasted_iota(jnp.int32, (rows, blk), 0)
    c = lax.broadcasted_iota(jnp.int32, (rows, blk), 1)
    diag_mask = c < (r & (blk - 1))
    no_carry = jnp.zeros((rows, 1), F32)

    def q_rows(b, t):
        q2 = q_ref[0, b * blk:(b + 1) * blk, t * LANES:(t + 1) * LANES]
        zero = jnp.zeros_like(q2)
        return jnp.concatenate([jnp.where(lower, q2, zero), jnp.where(lower, zero, q2)], axis=0)

    def kv(j, t):
        start = pl.multiple_of(j * blk, blk)
        lanes = slice(t * LANES, (t + 1) * LANES)
        return k_ref[0, pl.ds(start, blk), lanes], v_ref[0, pl.ds(start, blk), lanes]

    def chain(q, kblk, vblk, mask, carry_of):
        z = _scores(q, kblk, False)
        yield None
        sp = _fail_bits(z, mask)
        yield _row_sum(sp)
        upto = _upto(sp, tri)
        yield None
        a = _weights(z, upto, carry_of(), mask)
        yield None
        yield _apply(a, vblk, False)

    n_stages = 5

    def newest_two():
        qis = [pl.program_id(1) * n_q + b for b in range(n_q)]
        qs, chains = [], []
        for b in range(n_q):
            for t in range(n_tiles):
                q = q_rows(b, t)
                kd, vd = kv(qis[b], t)
                kp, vp = kv(jnp.maximum(qis[b] - 1, 0), t)
                diag = (chain(q, kd, vd, diag_mask, lambda: no_carry), [])
                prev = (chain(q, kp, vp, None, lambda got=diag[1]: got[1]), [])
                qs.append(q)
                chains += [diag, prev]
        for step in range(len(chains) + n_stages - 1):
            for i, (stages, got) in enumerate(chains):
                if 0 <= step - i < n_stages:
                    got.append(next(stages))
        started = []
        for b in range(n_q):
            has_prev = qis[b] > 0
            accs, carries = [], []
            for t in range(n_tiles):
                diag, prev = (got for _, got in chains[2 * (b * n_tiles + t):][:2])
                accs.append(diag[-1] + jnp.where(has_prev, prev[-1], 0.0))
                carries.append(diag[1] + jnp.where(has_prev, prev[1], 0.0))
            started.append((qis[b], qs[b * n_tiles:(b + 1) * n_tiles], accs, carries))
        return started

    def live(carries):
        return jnp.min(functools.reduce(jnp.minimum, carries)) < DEAD_LOG2

    def older(qi, qs, accs, carries):
        def cond(state):
            j, _, _, alive = state
            return jnp.logical_and(j >= 0, alive)

        def body(state):
            j, accs, carries, _ = state
            new_accs, new_carries = [], []
            for t in range(n_tiles):
                kblk, vblk = kv(j, t)
                z, sp, upto = _stick_scores(qs[t], kblk, tri, None, False)
                new_accs.append(accs[t] + _stick_values(z, upto, carries[t], vblk, None, False))
                new_carries.append(carries[t] + _row_sum(sp))
            return j - 1, tuple(new_accs), tuple(new_carries), live(new_carries)

        return lax.while_loop(cond, body, (qi - 2, tuple(accs), tuple(carries), True))[1]

    def write(b, accs):
        for t in range(n_tiles):
            o_ref[0, b * blk:(b + 1) * blk, t * LANES:(t + 1) * LANES] = jnp.where(
                lower, accs[t][:blk], accs[t][blk:]).astype(o_ref.dtype)

    for b, (qi, qs, accs, carries) in enumerate(newest_two()):
        write(b, accs)

        @pl.when(jnp.logical_and(qi >= 2, live(carries)))
        def _():
            write(b, older(qi, qs, accs, carries))


def _tri(blk):
    j = lax.broadcasted_iota(jnp.int32, (blk, blk), 0)
    s = lax.broadcasted_iota(jnp.int32, (blk, blk), 1)
    return (j >= s).astype(BF16)


def _attn_prompt(q, kb, vb, blk, q_blocks):
    b, t, _ = q.shape
    qspec = pl.BlockSpec((1, q_blocks * blk, ATTN_WIDTH), lambda bi, qi: (bi, qi, 0))
    kvspec = pl.BlockSpec((1, t, ATTN_WIDTH), lambda bi, qi: (bi, 0, 0))
    return pl.pallas_call(
        functools.partial(_attn_prompt_kernel, blk=blk),
        grid=(b, t // (q_blocks * blk)),
        in_specs=[qspec, kvspec, kvspec, _const_spec((blk, blk))],
        out_specs=qspec,
        out_shape=jax.ShapeDtypeStruct((b, t, ATTN_WIDTH), BF16),
        compiler_params=pltpu.CompilerParams(
            dimension_semantics=("arbitrary", "arbitrary"),
            vmem_limit_bytes=VMEM_LIMIT),
        name="attn_prompt",
    )(q, kb, vb, _tri(blk))


def _attn_sample_kernel(q_ref, kn_ref, vn_ref, ck_ref, cv_ref, tri_ref, o_ref, carry_ref, *, blk, new_blk):
    tq = q_ref.shape[1]
    rows = N_HEADS * tq
    q = q_ref[0]
    qrep = jnp.concatenate([q] * N_HEADS, axis=0)
    r = lax.broadcasted_iota(jnp.int32, (rows, ATTN_WIDTH), 0)
    c = lax.broadcasted_iota(jnp.int32, (rows, ATTN_WIDTH), 1)
    tq_bits = tq.bit_length() - 1
    head_bits = HEAD_DIM.bit_length() - 1
    assert tq == 1 << tq_bits
    head_sel = (r >> tq_bits) == (c >> head_bits)
    qrows = jnp.where(head_sel, qrep, jnp.zeros_like(qrep))
    tri = tri_ref[...]

    pad = jnp.zeros((new_blk - tq, ATTN_WIDTH), BF16)
    kblk = jnp.concatenate([kn_ref[0], pad], axis=0)
    vblk = jnp.concatenate([vn_ref[0], pad], axis=0)
    rn = lax.broadcasted_iota(jnp.int32, (rows, new_blk), 0)
    cn = lax.broadcasted_iota(jnp.int32, (rows, new_blk), 1)
    new_mask = cn < (rn & (tq - 1))
    tri_new = tri[:new_blk, :new_blk]
    blocks = [_stick_scores(qrows, kblk, tri_new, new_mask, False) + (vblk, new_mask, False)]
    window = ck_ref.shape[2]
    for j in reversed(range(window // blk)):
        kblk = ck_ref[0, :, j * blk:(j + 1) * blk].astype(BF16)
        vblk = cv_ref[0, :, j * blk:(j + 1) * blk].astype(BF16)
        blocks.append(_stick_scores(qrows, kblk, tri, None, True) + (vblk, None, True))
    carry = jnp.zeros((rows, 1), F32)
    parts = []
    for z, sp, upto, vblk, mask, v_transposed in blocks:
        parts.append(_stick_values(z, upto, carry, vblk, mask, v_transposed))
        carry = carry + _row_sum(sp)
    while len(parts) > 1:
        parts = [functools.reduce(jnp.add, parts[i:i + 2]) for i in range(0, len(parts), 2)]
    acc = parts[0]

    co = lax.broadcasted_iota(jnp.int32, (tq, ATTN_WIDTH), 1) >> head_bits
    out = jnp.zeros((tq, ATTN_WIDTH), F32)
    for h in range(N_HEADS):
        out = jnp.where(co == h, acc[h * tq:(h + 1) * tq], out)
    o_ref[0] = out.astype(o_ref.dtype)
    carry_ref[0] = jnp.broadcast_to(jnp.min(carry, axis=0, keepdims=True), carry_ref.shape[1:])


def _attn_sample_window(q, kb, vb, cache_k, cache_v, blk, new_blk, window):
    b, tq, _ = q.shape
    past = cache_k.shape[2]
    new = pl.BlockSpec((1, tq, ATTN_WIDTH), lambda bi: (bi, 0, 0))
    old = pl.BlockSpec((1, ATTN_WIDTH, window), lambda bi: (bi, 0, past // window - 1))
    low = pl.BlockSpec((1, 8, LANES), lambda bi: (bi, 0, 0))
    return pl.pallas_call(
        functools.partial(_attn_sample_kernel, blk=blk, new_blk=new_blk),
        grid=(b,),
        in_specs=[new, new, new, old, old, _const_spec((blk, blk))],
        out_specs=[new, low],
        out_shape=[jax.ShapeDtypeStruct((b, tq, ATTN_WIDTH), BF16),
                   jax.ShapeDtypeStruct((b, 8, LANES), F32)],
        compiler_params=pltpu.CompilerParams(dimension_semantics=("arbitrary",),
                                             vmem_limit_bytes=VMEM_LIMIT),
        name="attn_sample",
    )(q, kb, vb, cache_k, cache_v, _tri(blk))


def _attn_sample(q, kb, vb, cache_k, cache_v, blk, new_blk):
    past = cache_k.shape[2]
    recent, min_carry = _attn_sample_window(q, kb, vb, cache_k, cache_v, blk, new_blk, blk)
    if past == blk:
        return recent
    return lax.cond(jnp.min(min_carry) < DEAD_LOG2,
                    lambda: _attn_sample_window(q, kb, vb, cache_k, cache_v, blk, new_blk, past)[0],
                    lambda: recent)


def _with_history(hist, cur):
    nb, h, c = hist.shape
    t = cur.shape[1]
    return jnp.concatenate([hist, cur], axis=1).reshape(nb * (h + t), c)


def _drop_history(flat, nb, h, t):
    c = flat.shape[-1]
    return flat.reshape(nb, h + t, c)[:, h:, :].reshape(nb * t, c)


def _twice_gelu_tanh(x):
    c = 0.7978845608028654
    return x * (1.0 + jnp.tanh(x * (c + (c * 0.044715) * (x * x))))


def _post_kernel(x_ref, attn_ref, u_ref, ga_ref, gb_ref, pleft_ref, cleft_ref,
                 wa_ref, wb_ref, pw_ref, ps_ref, wo_ref, nffn_ref, wup_ref, cw_ref, cb_ref,
                 wdn_ref, nfin_ref,
                 y_ref, ptail_ref, ctail_ref,
                 uhist_ref, chist_ref, h2_ref, act_ref, *, pos0):
    nb, tt, _ = x_ref.shape
    m = nb * tt
    ti = pl.program_id(1)

    @pl.when(ti == 0)
    def _():
        uhist_ref[...] = pleft_ref[...]
        chist_ref[...] = cleft_ref[...]

    branch_a = jnp.dot(attn_ref[...].reshape(m, ATTN_WIDTH), wa_ref[...], preferred_element_type=F32)
    x = x_ref[...].reshape(m, D_MODEL)
    u3 = u_ref[...]
    u = u3.reshape(m, POOL_WIDTH)

    ext = _with_history(uhist_ref[...], u3)
    row = lax.broadcasted_iota(jnp.int32, (m, POOL_GROUP), 0)
    pos = pos0 + ti * tt + (row & (tt - 1))
    pmix = []
    for g, w in enumerate(POOL_WINDOWS):
        sl = slice(g * POOL_GROUP, (g + 1) * POOL_GROUP)
        s = ext[:, sl]
        shift = 1
        while shift < w:
            s = s + pltpu.roll(s, shift, axis=0)
            shift *= 2
        win = _drop_history(s, nb, POOL_HALO, tt)
        cnt = jnp.minimum(pos + 1, w).astype(F32)
        pooled = win / cnt - u[:, sl]
        pmix.append(jnp.dot(pooled.astype(BF16), pw_ref[g], preferred_element_type=F32))
    pmix = jnp.concatenate(pmix, axis=1) * ps_ref[...]
    branch_b = jnp.dot(pmix.astype(BF16), wb_ref[...], preferred_element_type=F32)
    merged = (ga_ref[...].reshape(m, D_MODEL).astype(F32) * branch_a
              + gb_ref[...].reshape(m, D_MODEL).astype(F32) * branch_b)
    x1 = x + jnp.dot(merged.astype(BF16), wo_ref[...], preferred_element_type=F32)

    tail = u3[:, tt - POOL_HALO:, :]
    uhist_ref[...] = tail
    ptail_ref[...] = tail[:, POOL_HALO - POOL_STATE:, :]

    y_ref[...] = x1.reshape(nb, tt, D_MODEL)
    h2_ref[...] = _rmsnorm(x1, nffn_ref[...]).astype(BF16)
    n_chunks = D_FF // FF_CHUNK

    def up_proj(j):
        cols = [slice(half * D_FF + j * FF_CHUNK, half * D_FF + (j + 1) * FF_CHUNK) for half in range(2)]
        return [(cs, jnp.dot(h2_ref[...], wup_ref[:, cs], preferred_element_type=F32)) for cs in cols]

    ups = up_proj(0)
    for j in range(n_chunks):
        cur, ups = ups, (up_proj(j + 1) if j + 1 < n_chunks else None)
        halves = []
        for cs, up in cur:
            up3 = up.reshape(nb, tt, FF_CHUNK)
            e = _with_history(chist_ref[:, :, cs], up3)
            conv = cb_ref[:, cs] + up * cw_ref[2:3, cs]
            for d in (1, 2):
                shifted = _drop_history(pltpu.roll(e, d, axis=0), nb, CONV_HALO, tt)
                conv = conv + shifted * cw_ref[2 - d:3 - d, cs]
            halves.append(conv)
            ctile = up3[:, tt - CONV_HALO:, :]
            chist_ref[:, :, cs] = ctile
            ctail_ref[:, :, cs] = ctile[:, CONV_HALO - (CONV_WIDTH - 1):, :]
        act = (_twice_gelu_tanh(halves[0]) * halves[1]).astype(BF16)
        act_ref[:, j * FF_CHUNK:(j + 1) * FF_CHUNK] = act
    down = jnp.dot(act_ref[...], wdn_ref[...], preferred_element_type=F32)
    y = _rmsnorm(y_ref[...].reshape(m, D_MODEL) + down, nfin_ref[...])
    y_ref[...] = y.reshape(nb, tt, D_MODEL)


def _post(x, attn, u, ga, gb, pool_left, conv_left, weights, nb, tt, pos0):
    b, t, _ = x.shape
    (w_a, w_b, pool_w, pool_scale, w_o, norm_ffn, w_up, conv_w, conv_b, w_down, norm_final) = weights
    tile = lambda w: pl.BlockSpec((nb, tt, w), lambda bi, ti: (bi, ti, 0))
    per_seq = lambda r, w: pl.BlockSpec((nb, r, w), lambda bi, ti: (bi, 0, 0))
    consts = [w_a, w_b, pool_w, pool_scale, w_o, norm_ffn, w_up, conv_w, conv_b, w_down, norm_final]
    return pl.pallas_call(
        functools.partial(_post_kernel, pos0=pos0),
        grid=(b // nb, t // tt),
        in_specs=[tile(D_MODEL), tile(ATTN_WIDTH), tile(POOL_WIDTH), tile(D_MODEL), tile(D_MODEL),
                  per_seq(POOL_HALO, POOL_WIDTH), per_seq(CONV_HALO, 2 * D_FF)]
                 + [_const_spec(c.shape) for c in consts],
        out_specs=[tile(D_MODEL), per_seq(POOL_STATE, POOL_WIDTH), per_seq(CONV_WIDTH - 1, 2 * D_FF)],
        out_shape=[jax.ShapeDtypeStruct((b, t, D_MODEL), F32),
                   jax.ShapeDtypeStruct((b, POOL_STATE, POOL_WIDTH), F32),
                   jax.ShapeDtypeStruct((b, CONV_WIDTH - 1, 2 * D_FF), F32)],
        scratch_shapes=[pltpu.VMEM((nb, POOL_HALO, POOL_WIDTH), F32),
                        pltpu.VMEM((nb, CONV_HALO, 2 * D_FF), F32),
                        pltpu.VMEM((nb * tt, D_MODEL), BF16),
                        pltpu.VMEM((nb * tt, D_FF), BF16)],
        compiler_params=pltpu.CompilerParams(dimension_semantics=("arbitrary", "arbitrary"),
                                             vmem_limit_bytes=VMEM_LIMIT),
        name="post",
    )(x, attn, u, ga, gb, pool_left, conv_left, *consts)


def _left_pad(state, rows):
    b, r, c = state.shape
    return jnp.concatenate([jnp.zeros((b, rows - r, c), state.dtype), state], axis=1)


def _stream(x, past, pool_left, conv_left, norm_mix, w_in_bf, post_weights, *, tm, nb, tt, pos0):
    b, t, _ = x.shape
    m = b * t
    kv_transposed = past is None
    q, k, v, u, kb, vb, ga, gb = _inproj(x.reshape(m, D_MODEL), norm_mix, w_in_bf, tm, t, kv_transposed)
    seq = lambda a: a.reshape(b, t, a.shape[-1])
    if past is None:
        attn = _attn_prompt(seq(q), seq(kb), seq(vb), blk=256, q_blocks=2)
        heads = lambda a: a.reshape(1, b, N_HEADS, HEAD_DIM, t).transpose(0, 1, 4, 2, 3)
    else:
        time_last = lambda c: c.transpose(0, 2, 3, 1).reshape(b, ATTN_WIDTH, c.shape[1])
        attn = _attn_sample(seq(q), seq(kb), seq(vb), time_last(past[0]), time_last(past[1]),
                            blk=256, new_blk=128)
        heads = lambda a: a.reshape(1, b, t, N_HEADS, HEAD_DIM)
    y, ptail, ctail = _post(x, attn, seq(u), seq(ga), seq(gb),
                            _left_pad(pool_left, POOL_HALO), _left_pad(conv_left, CONV_HALO),
                            post_weights, nb, tt, pos0)
    return (y, heads(k), heads(v),
            ptail[None], ctail[None])


def kernel(x_prompt, x_sample, cache_k, cache_v, state_pool, state_conv, norm_mix, w_in, w_a, w_b,
           pool_w, pool_scale, w_o, norm_ffn, w_up, conv_w, conv_b, w_down, norm_final):
    assert w_in.shape[0] == 1, "single-layer trunk"
    row = lambda a: a.reshape(1, -1)
    post_weights = (w_a[0].astype(BF16), w_b[0].astype(BF16), pool_w[0].astype(BF16),
                    row(pool_scale[0]), w_o[0].astype(BF16), row(norm_ffn[0]),
                    w_up[0].astype(BF16), conv_w[0], row(conv_b[0]), (0.5 * w_down[0]).astype(BF16),
                    row(norm_final))
    w_in_bf = w_in[0].astype(BF16)
    bp = x_prompt.shape[0]
    zeros_pool = jnp.zeros((bp, POOL_STATE, POOL_WIDTH), F32)
    zeros_conv = jnp.zeros((bp, CONV_WIDTH - 1, 2 * D_FF), F32)
    yp, kp, vp, pp, cp = _stream(x_prompt, None, zeros_pool, zeros_conv, norm_mix[0], w_in_bf,
                                 post_weights, tm=512, nb=1, tt=512, pos0=0)
    bs, ts, _ = x_sample.shape
    ys, ks, vs, ps, cs = _stream(x_sample, (cache_k[0], cache_v[0]), state_pool[0], state_conv[0],
                                 norm_mix[0], w_in_bf, post_weights,
                                 tm=bs * ts, nb=bs, tt=ts, pos0=cache_k.shape[2])
    return (yp, ys, kp, vp, pp, cp, ks, vs, ps, cs)
```

```python
import functools

import jax
import jax.numpy as jnp
from jax import lax
from jax.experimental import pallas as pl
from jax.experimental.pallas import tpu as pltpu

D_MODEL = 1024
N_HEADS = 8
HEAD_DIM = 64
ATTN_WIDTH = N_HEADS * HEAD_DIM
POOL_WINDOWS = (2, 4, 8, 16)
POOL_WIDTH = 512
POOL_GROUP = POOL_WIDTH // len(POOL_WINDOWS)
POOL_STATE = max(POOL_WINDOWS) - 1
D_FF = 2816
CONV_WIDTH = 3
EPS = 1e-6
IN_WIDTH = 3 * ATTN_WIDTH + POOL_WIDTH + 2 * D_MODEL

LANES = 128
HEADS_PER_LANE_TILE = LANES // HEAD_DIM
POOL_HALO = 16
CONV_HALO = 8
FF_CHUNK = 256
VMEM_LIMIT = 56 * 1024 * 1024
LOG2E = 1.4426950408889634
DEAD_LOG2 = 160.0
SOFTPLUS_CLAMP = 126.0

F32 = jnp.float32
BF16 = jnp.bfloat16


def _rmsnorm(x, g):
    return x * lax.rsqrt(jnp.mean(x * x, axis=-1, keepdims=True) + EPS) * g


def _const_spec(shape):
    nd = len(shape)
    return pl.BlockSpec(shape, lambda *_: (0,) * nd, pipeline_mode=pl.Buffered(1))


def _inproj_kernel(x_ref, g_ref, w_ref, q_ref, k_ref, v_ref, u_ref, kb_ref, vb_ref, ga_ref, gb_ref,
                   *, kv_transposed):
    h = _rmsnorm(x_ref[...], g_ref[...]).astype(BF16)

    def proj(lo, hi):
        return jnp.dot(h, w_ref[:, lo:hi], preferred_element_type=F32)

    a = ATTN_WIDTH
    u0 = 3 * a
    g0 = u0 + POOL_WIDTH
    ga_ref[...] = jax.nn.sigmoid(proj(g0, g0 + D_MODEL)).astype(BF16)
    gb_ref[...] = jax.nn.sigmoid(proj(g0 + D_MODEL, g0 + 2 * D_MODEL)).astype(BF16)
    k = proj(a, 2 * a)
    kb_ref[...] = k.astype(BF16)
    v = proj(2 * a, 3 * a)
    vb_ref[...] = v.astype(BF16)
    if kv_transposed:
        k_ref[0] = k.T
        v_ref[0] = v.T
    else:
        k_ref[...] = k
        v_ref[...] = v
    q_ref[...] = (proj(0, a) * (HEAD_DIM ** -0.5 * LOG2E)).astype(BF16)
    u_ref[...] = proj(u0, u0 + POOL_WIDTH)


def _inproj(x2d, norm_mix, w_in_bf, tm, seq_len, kv_transposed):
    m = x2d.shape[0]
    row = lambda w: pl.BlockSpec((tm, w), lambda i: (i, 0))
    outs = [(ATTN_WIDTH, BF16), (ATTN_WIDTH, F32), (ATTN_WIDTH, F32), (POOL_WIDTH, F32),
            (ATTN_WIDTH, BF16), (ATTN_WIDTH, BF16), (D_MODEL, BF16), (D_MODEL, BF16)]
    out_specs = [row(w) for w, _ in outs]
    out_shape = [jax.ShapeDtypeStruct((m, w), dt) for w, dt in outs]
    if kv_transposed:
        tiles = seq_len // tm
        for i in (1, 2):
            out_specs[i] = pl.BlockSpec((1, ATTN_WIDTH, tm), lambda i: (i // tiles, 0, i % tiles))
            out_shape[i] = jax.ShapeDtypeStruct((m // seq_len, ATTN_WIDTH, seq_len), F32)
    return pl.pallas_call(
        functools.partial(_inproj_kernel, kv_transposed=kv_transposed),
        grid=(m // tm,),
        in_specs=[row(D_MODEL), _const_spec((1, D_MODEL)), _const_spec((D_MODEL, IN_WIDTH))],
        out_specs=out_specs,
        out_shape=out_shape,
        compiler_params=pltpu.CompilerParams(dimension_semantics=("arbitrary",),
                                             vmem_limit_bytes=VMEM_LIMIT),
        name="inproj",
    )(x2d, norm_mix.reshape(1, D_MODEL), w_in_bf)


_NT = (((1,), (1,)), ((), ()))


def _softplus2(z):
    return jnp.maximum(jnp.log2(1.0 + jnp.exp2(jnp.minimum(z, SOFTPLUS_CLAMP))), z)


def _scores(qrows, kblk, k_transposed):
    if k_transposed:
        return jnp.dot(qrows, kblk, preferred_element_type=F32)
    return lax.dot_general(qrows, kblk, _NT, preferred_element_type=F32)


def _fail_bits(z, mask):
    sp = _softplus2(z)
    return sp if mask is None else jnp.where(mask, sp, 0.0)


def _upto(sp, tri):
    return jnp.dot(sp.astype(BF16), tri, preferred_element_type=F32)


def _weights(z, upto, carry, mask):
    a = jnp.exp2(z - upto - carry)
    if mask is not None:
        a = jnp.where(mask, a, 0.0)
    return a.astype(BF16)


def _apply(a, vblk, v_transposed):
    if v_transposed:
        return lax.dot_general(a, vblk, _NT, preferred_element_type=F32)
    return jnp.dot(a, vblk, preferred_element_type=F32)


def _stick_scores(qrows, kblk, tri, mask, k_transposed):
    z = _scores(qrows, kblk, k_transposed)
    sp = _fail_bits(z, mask)
    return z, sp, _upto(sp, tri)


def _stick_values(z, upto, carry, vblk, mask, v_transposed):
    return _apply(_weights(z, upto, carry, mask), vblk, v_transposed)


def _row_sum(sp):
    return jnp.sum(sp, axis=1, keepdims=True)


N_CHAIN_STAGES = 5


def _chain(q, kblk, vblk, tri, mask, carry_of, transposed):
    z = _scores(q, kblk, transposed)
    yield None
    sp = _fail_bits(z, mask)
    yield _row_sum(sp)
    upto = _upto(sp, tri)
    yield None
    a = _weights(z, upto, carry_of(), mask)
    yield None
    yield _apply(a, vblk, transposed)


def _wavefront(chains):
    for step in range(len(chains) + N_CHAIN_STAGES - 1):
        for i, (stages, got) in enumerate(chains):
            if 0 <= step - i < N_CHAIN_STAGES:
                got.append(next(stages))


def _attn_prompt_kernel(q_ref, k_ref, v_ref, tri_ref, o_ref, *, blk):
    n_q = q_ref.shape[1] // blk
    n_tiles = q_ref.shape[2] // LANES
    rows = 2 * blk
    tri = tri_ref[...]
    lower = lax.broadcasted_iota(jnp.int32, (blk, LANES), 1) < HEAD_DIM
    r = lax.broadcasted_iota(jnp.int32, (rows, blk), 0)
    c = lax.broadcasted_iota(jnp.int32, (rows, blk), 1)
    diag_mask = c < (r & (blk - 1))
    no_carry = jnp.zeros((rows, 1), F32)

    def q_rows(b, t):
        q2 = q_ref[0, b * blk:(b + 1) * blk, t * LANES:(t + 1) * LANES]
        zero = jnp.zeros_like(q2)
        return jnp.concatenate([jnp.where(lower, q2, zero), jnp.where(lower, zero, q2)], axis=0)

    def kv(j, t):
        start = pl.multiple_of(j * blk, blk)
        lanes = slice(t * LANES, (t + 1) * LANES)
        return k_ref[0, pl.ds(start, blk), lanes], v_ref[0, pl.ds(start, blk), lanes]

    def newest_two():
        qis = [pl.program_id(1) * n_q + b for b in range(n_q)]
        qs, chains = [], []
        for b in range(n_q):
            for t in range(n_tiles):
                q = q_rows(b, t)
                kd, vd = kv(qis[b], t)
                kp, vp = kv(jnp.maximum(qis[b] - 1, 0), t)
                diag = (_chain(q, kd, vd, tri, diag_mask, lambda: no_carry, False), [])
                prev = (_chain(q, kp, vp, tri, None, lambda got=diag[1]: got[1], False), [])
                qs.append(q)
                chains += [diag, prev]
        _wavefront(chains)
        started = []
        for b in range(n_q):
            has_prev = qis[b] > 0
            accs, carries = [], []
            for t in range(n_tiles):
                diag, prev = (got for _, got in chains[2 * (b * n_tiles + t):][:2])
                accs.append(diag[-1] + jnp.where(has_prev, prev[-1], 0.0))
                carries.append(diag[1] + jnp.where(has_prev, prev[1], 0.0))
            started.append((qis[b], qs[b * n_tiles:(b + 1) * n_tiles], accs, carries))
        return started

    def live(carries):
        return jnp.min(functools.reduce(jnp.minimum, carries)) < DEAD_LOG2

    def older(qi, qs, accs, carries):
        def cond(state):
            j, _, _, alive = state
            return jnp.logical_and(j >= 0, alive)

        def body(state):
            j, accs, carries, _ = state
            new_accs, new_carries = [], []
            for t in range(n_tiles):
                kblk, vblk = kv(j, t)
                z, sp, upto = _stick_scores(qs[t], kblk, tri, None, False)
                new_accs.append(accs[t] + _stick_values(z, upto, carries[t], vblk, None, False))
                new_carries.append(carries[t] + _row_sum(sp))
            return j - 1, tuple(new_accs), tuple(new_carries), live(new_carries)

        return lax.while_loop(cond, body, (qi - 2, tuple(accs), tuple(carries), True))[1]

    def write(b, accs):
        for t in range(n_tiles):
            o_ref[0, b * blk:(b + 1) * blk, t * LANES:(t + 1) * LANES] = jnp.where(
                lower, accs[t][:blk], accs[t][blk:]).astype(o_ref.dtype)

    for b, (qi, qs, accs, carries) in enumerate(newest_two()):
        write(b, accs)

        @pl.when(jnp.logical_and(qi >= 2, live(carries)))
        def _():
            write(b, older(qi, qs, accs, carries))


def _tri(blk):
    j = lax.broadcasted_iota(jnp.int32, (blk, blk), 0)
    s = lax.broadcasted_iota(jnp.int32, (blk, blk), 1)
    return (j >= s).astype(BF16)


def _attn_prompt(q, kb, vb, blk, q_blocks):
    b, t, _ = q.shape
    qspec = pl.BlockSpec((1, q_blocks * blk, ATTN_WIDTH), lambda bi, qi: (bi, qi, 0))
    kvspec = pl.BlockSpec((1, t, ATTN_WIDTH), lambda bi, qi: (bi, 0, 0))
    return pl.pallas_call(
        functools.partial(_attn_prompt_kernel, blk=blk),
        grid=(b, t // (q_blocks * blk)),
        in_specs=[qspec, kvspec, kvspec, _const_spec((blk, blk))],
        out_specs=qspec,
        out_shape=jax.ShapeDtypeStruct((b, t, ATTN_WIDTH), BF16),
        compiler_params=pltpu.CompilerParams(
            dimension_semantics=("arbitrary", "arbitrary"),
            vmem_limit_bytes=VMEM_LIMIT),
        name="attn_prompt",
    )(q, kb, vb, _tri(blk))


def _attn_sample_kernel(q_ref, kn_ref, vn_ref, ck_ref, cv_ref, tri_ref, o_ref, carry_ref, *, blk, new_blk):
    n_seq, tq, _ = q_ref.shape
    rows = N_HEADS * tq
    r = lax.broadcasted_iota(jnp.int32, (rows, ATTN_WIDTH), 0)
    c = lax.broadcasted_iota(jnp.int32, (rows, ATTN_WIDTH), 1)
    tq_bits = tq.bit_length() - 1
    head_bits = HEAD_DIM.bit_length() - 1
    assert tq == 1 << tq_bits
    head_sel = (r >> tq_bits) == (c >> head_bits)
    tri = tri_ref[...]
    tri_new = tri[:new_blk, :new_blk]
    rn = lax.broadcasted_iota(jnp.int32, (rows, new_blk), 0)
    cn = lax.broadcasted_iota(jnp.int32, (rows, new_blk), 1)
    new_mask = cn < (rn & (tq - 1))
    pad = jnp.zeros((new_blk - tq, ATTN_WIDTH), BF16)
    no_carry = jnp.zeros((rows, 1), F32)
    co = lax.broadcasted_iota(jnp.int32, (tq, ATTN_WIDTH), 1) >> head_bits
    window = ck_ref.shape[2]

    def carry_after(earlier):
        return lambda: functools.reduce(jnp.add, [got[1] for got in earlier], no_carry)

    per_seq, chains = [], []
    for s in range(n_seq):
        q = q_ref[s]
        qrep = jnp.concatenate([q] * N_HEADS, axis=0)
        qrows = jnp.where(head_sel, qrep, jnp.zeros_like(qrep))
        mine = [(_chain(qrows, jnp.concatenate([kn_ref[s], pad], axis=0),
                        jnp.concatenate([vn_ref[s], pad], axis=0), tri_new, new_mask,
                        lambda: no_carry, False), [])]
        for j in reversed(range(window // blk)):
            kblk = ck_ref[s, :, j * blk:(j + 1) * blk].astype(BF16)
            vblk = cv_ref[s, :, j * blk:(j + 1) * blk].astype(BF16)
            mine.append((_chain(qrows, kblk, vblk, tri, None,
                                carry_after([got for _, got in mine]), True), []))
        per_seq.append(mine)
        chains += mine
    _wavefront(chains)

    for s, mine in enumerate(per_seq):
        parts = [got[-1] for _, got in mine]
        while len(parts) > 1:
            parts = [functools.reduce(jnp.add, parts[i:i + 2]) for i in range(0, len(parts), 2)]
        acc = parts[0]
        out = jnp.zeros((tq, ATTN_WIDTH), F32)
        for h in range(N_HEADS):
            out = jnp.where(co == h, acc[h * tq:(h + 1) * tq], out)
        o_ref[s] = out.astype(o_ref.dtype)
        carry = carry_after([got for _, got in mine])()
        carry_ref[s] = jnp.broadcast_to(jnp.min(carry, axis=0, keepdims=True), carry_ref.shape[1:])


def _attn_sample_window(q, kb, vb, cache_k, cache_v, blk, new_blk, window, n_seq):
    b, tq, _ = q.shape
    past = cache_k.shape[2]
    new = pl.BlockSpec((n_seq, tq, ATTN_WIDTH), lambda bi: (bi, 0, 0))
    old = pl.BlockSpec((n_seq, ATTN_WIDTH, window), lambda bi: (bi, 0, past // window - 1))
    low = pl.BlockSpec((n_seq, 8, LANES), lambda bi: (bi, 0, 0))
    return pl.pallas_call(
        functools.partial(_attn_sample_kernel, blk=blk, new_blk=new_blk),
        grid=(b // n_seq,),
        in_specs=[new, new, new, old, old, _const_spec((blk, blk))],
        out_specs=[new, low],
        out_shape=[jax.ShapeDtypeStruct((b, tq, ATTN_WIDTH), BF16),
                   jax.ShapeDtypeStruct((b, 8, LANES), F32)],
        compiler_params=pltpu.CompilerParams(dimension_semantics=("arbitrary",),
                                             vmem_limit_bytes=VMEM_LIMIT),
        name="attn_sample",
    )(q, kb, vb, cache_k, cache_v, _tri(blk))


def _attn_sample(q, kb, vb, cache_k, cache_v, blk, new_blk):
    past = cache_k.shape[2]
    recent, min_carry = _attn_sample_window(q, kb, vb, cache_k, cache_v, blk, new_blk, blk, 4)
    if past == blk:
        return recent
    return lax.cond(jnp.min(min_carry) < DEAD_LOG2,
                    lambda: _attn_sample_window(q, kb, vb, cache_k, cache_v, blk, new_blk, past, 2)[0],
                    lambda: recent)


def _with_history(hist, cur):
    nb, h, c = hist.shape
    t = cur.shape[1]
    return jnp.concatenate([hist, cur], axis=1).reshape(nb * (h + t), c)


def _drop_history(flat, nb, h, t):
    c = flat.shape[-1]
    return flat.reshape(nb, h + t, c)[:, h:, :].reshape(nb * t, c)


def _twice_gelu_tanh(x):
    c = 0.7978845608028654
    return x * (1.0 + jnp.tanh(x * (c + (c * 0.044715) * (x * x))))


def _post_kernel(x_ref, attn_ref, u_ref, ga_ref, gb_ref, pleft_ref, cleft_ref,
                 wa_ref, wb_ref, pw_ref, ps_ref, wo_ref, nffn_ref, wup_ref, cw_ref, cb_ref,
                 wdn_ref, nfin_ref,
                 y_ref, ptail_ref, ctail_ref,
                 uhist_ref, chist_ref, h2_ref, act_ref, *, pos0):
    nb, tt, _ = x_ref.shape
    m = nb * tt
    ti = pl.program_id(1)

    @pl.when(ti == 0)
    def _():
        uhist_ref[...] = pleft_ref[...]
        chist_ref[...] = cleft_ref[...]

    branch_a = jnp.dot(attn_ref[...].reshape(m, ATTN_WIDTH), wa_ref[...], preferred_element_type=F32)
    x = x_ref[...].reshape(m, D_MODEL)
    u3 = u_ref[...]
    u = u3.reshape(m, POOL_WIDTH)

    ext = _with_history(uhist_ref[...], u3)
    row = lax.broadcasted_iota(jnp.int32, (m, POOL_GROUP), 0)
    pos = pos0 + ti * tt + (row & (tt - 1))
    pmix = []
    for g, w in enumerate(POOL_WINDOWS):
        sl = slice(g * POOL_GROUP, (g + 1) * POOL_GROUP)
        s = ext[:, sl]
        shift = 1
        while shift < w:
            s = s + pltpu.roll(s, shift, axis=0)
            shift *= 2
        win = _drop_history(s, nb, POOL_HALO, tt)
        cnt = jnp.minimum(pos + 1, w).astype(F32)
        pooled = win / cnt - u[:, sl]
        pmix.append(jnp.dot(pooled.astype(BF16), pw_ref[g], preferred_element_type=F32))
    pmix = jnp.concatenate(pmix, axis=1) * ps_ref[...]
    branch_b = jnp.dot(pmix.astype(BF16), wb_ref[...], preferred_element_type=F32)
    merged = (ga_ref[...].reshape(m, D_MODEL).astype(F32) * branch_a
              + gb_ref[...].reshape(m, D_MODEL).astype(F32) * branch_b)
    x1 = x + jnp.dot(merged.astype(BF16), wo_ref[...], preferred_element_type=F32)

    tail = u3[:, tt - POOL_HALO:, :]
    uhist_ref[...] = tail
    ptail_ref[...] = tail[:, POOL_HALO - POOL_STATE:, :]

    y_ref[...] = x1.reshape(nb, tt, D_MODEL)
    h2_ref[...] = _rmsnorm(x1, nffn_ref[...]).astype(BF16)
    n_chunks = D_FF // FF_CHUNK

    def up_proj(j):
        cols = [slice(half * D_FF + j * FF_CHUNK, half * D_FF + (j + 1) * FF_CHUNK) for half in range(2)]
        return [(cs, jnp.dot(h2_ref[...], wup_ref[:, cs], preferred_element_type=F32)) for cs in cols]

    ups = up_proj(0)
    for j in range(n_chunks):
        cur, ups = ups, (up_proj(j + 1) if j + 1 < n_chunks else None)
        halves = []
        for cs, up in cur:
            up3 = up.reshape(nb, tt, FF_CHUNK)
            e = _with_history(chist_ref[:, :, cs], up3)
            conv = cb_ref[:, cs] + up * cw_ref[2:3, cs]
            for d in (1, 2):
                shifted = _drop_history(pltpu.roll(e, d, axis=0), nb, CONV_HALO, tt)
                conv = conv + shifted * cw_ref[2 - d:3 - d, cs]
            halves.append(conv)
            ctile = up3[:, tt - CONV_HALO:, :]
            chist_ref[:, :, cs] = ctile
            ctail_ref[:, :, cs] = ctile[:, CONV_HALO - (CONV_WIDTH - 1):, :]
        act = (_twice_gelu_tanh(halves[0]) * halves[1]).astype(BF16)
        act_ref[:, j * FF_CHUNK:(j + 1) * FF_CHUNK] = act
    down = jnp.dot(act_ref[...], wdn_ref[...], preferred_element_type=F32)
    y = _rmsnorm(y_ref[...].reshape(m, D_MODEL) + down, nfin_ref[...])
    y_ref[...] = y.reshape(nb, tt, D_MODEL)


def _post(x, attn, u, ga, gb, pool_left, conv_left, weights, nb, tt, pos0):
    b, t, _ = x.shape
    (w_a, w_b, pool_w, pool_scale, w_o, norm_ffn, w_up, conv_w, conv_b, w_down, norm_final) = weights
    tile = lambda w: pl.BlockSpec((nb, tt, w), lambda bi, ti: (bi, ti, 0))
    per_seq = lambda r, w: pl.BlockSpec((nb, r, w), lambda bi, ti: (bi, 0, 0))
    consts = [w_a, w_b, pool_w, pool_scale, w_o, norm_ffn, w_up, conv_w, conv_b, w_down, norm_final]
    return pl.pallas_call(
        functools.partial(_post_kernel, pos0=pos0),
        grid=(b // nb, t // tt),
        in_specs=[tile(D_MODEL), tile(ATTN_WIDTH), tile(POOL_WIDTH), tile(D_MODEL), tile(D_MODEL),
                  per_seq(POOL_HALO, POOL_WIDTH), per_seq(CONV_HALO, 2 * D_FF)]
                 + [_const_spec(c.shape) for c in consts],
        out_specs=[tile(D_MODEL), per_seq(POOL_STATE, POOL_WIDTH), per_seq(CONV_WIDTH - 1, 2 * D_FF)],
        out_shape=[jax.ShapeDtypeStruct((b, t, D_MODEL), F32),
                   jax.ShapeDtypeStruct((b, POOL_STATE, POOL_WIDTH), F32),
                   jax.ShapeDtypeStruct((b, CONV_WIDTH - 1, 2 * D_FF), F32)],
        scratch_shapes=[pltpu.VMEM((nb, POOL_HALO, POOL_WIDTH), F32),
                        pltpu.VMEM((nb, CONV_HALO, 2 * D_FF), F32),
                        pltpu.VMEM((nb * tt, D_MODEL), BF16),
                        pltpu.VMEM((nb * tt, D_FF), BF16)],
        compiler_params=pltpu.CompilerParams(dimension_semantics=("arbitrary", "arbitrary"),
                                             vmem_limit_bytes=VMEM_LIMIT),
        name="post",
    )(x, attn, u, ga, gb, pool_left, conv_left, *consts)


def _left_pad(state, rows):
    b, r, c = state.shape
    return jnp.concatenate([jnp.zeros((b, rows - r, c), state.dtype), state], axis=1)


def _stream(x, past, pool_left, conv_left, norm_mix, w_in_bf, post_weights, *, tm, nb, tt, pos0):
    b, t, _ = x.shape
    m = b * t
    kv_transposed = past is None
    q, k, v, u, kb, vb, ga, gb = _inproj(x.reshape(m, D_MODEL), norm_mix, w_in_bf, tm, t, kv_transposed)
    seq = lambda a: a.reshape(b, t, a.shape[-1])
    if past is None:
        attn = _attn_prompt(seq(q), seq(kb), seq(vb), blk=256, q_blocks=4)
        heads = lambda a: a.reshape(1, b, N_HEADS, HEAD_DIM, t).transpose(0, 1, 4, 2, 3)
    else:
        time_last = lambda c: c.transpose(0, 2, 3, 1).reshape(b, ATTN_WIDTH, c.shape[1])
        attn = _attn_sample(seq(q), seq(kb), seq(vb), time_last(past[0]), time_last(past[1]),
                            blk=256, new_blk=128)
        heads = lambda a: a.reshape(1, b, t, N_HEADS, HEAD_DIM)
    y, ptail, ctail = _post(x, attn, seq(u), seq(ga), seq(gb),
                            _left_pad(pool_left, POOL_HALO), _left_pad(conv_left, CONV_HALO),
                            post_weights, nb, tt, pos0)
    return (y, heads(k), heads(v),
            ptail[None], ctail[None])


def kernel(x_prompt, x_sample, cache_k, cache_v, state_pool, state_conv, norm_mix, w_in, w_a, w_b,
           pool_w, pool_scale, w_o, norm_ffn, w_up, conv_w, conv_b, w_down, norm_final):
    assert w_in.shape[0] == 1, "single-layer trunk"
    row = lambda a: a.reshape(1, -1)
    post_weights = (w_a[0].astype(BF16), w_b[0].astype(BF16), pool_w[0].astype(BF16),
                    row(pool_scale[0]), w_o[0].astype(BF16), row(norm_ffn[0]),
                    w_up[0].astype(BF16), conv_w[0], row(conv_b[0]), (0.5 * w_down[0]).astype(BF16),
                    row(norm_final))
    w_in_bf = w_in[0].astype(BF16)
    bp = x_prompt.shape[0]
    zeros_pool = jnp.zeros((bp, POOL_STATE, POOL_WIDTH), F32)
    zeros_conv = jnp.zeros((bp, CONV_WIDTH - 1, 2 * D_FF), F32)
    yp, kp, vp, pp, cp = _stream(x_prompt, None, zeros_pool, zeros_conv, norm_mix[0], w_in_bf,
                                 post_weights, tm=1024, nb=1, tt=512, pos0=0)
    bs, ts, _ = x_sample.shape
    ys, ks, vs, ps, cs = _stream(x_sample, (cache_k[0], cache_v[0]), state_pool[0], state_conv[0],
                                 norm_mix[0], w_in_bf, post_weights,
                                 tm=bs * ts, nb=bs, tt=ts, pos0=cache_k.shape[2])
    return (yp, ys, kp, vp, pp, cp, ks, vs, ps, cs)
```

```python
import functools

import jax
import jax.numpy as jnp
from jax import lax
from jax.experimental import pallas as pl
from jax.experimental.pallas import tpu as pltpu

D_MODEL = 1024
N_HEADS = 8
HEAD_DIM = 64
ATTN_WIDTH = N_HEADS * HEAD_DIM
POOL_WINDOWS = (2, 4, 8, 16)
POOL_WIDTH = 512
POOL_GROUP = POOL_WIDTH // len(POOL_WINDOWS)
POOL_STATE = max(POOL_WINDOWS) - 1
D_FF = 2816
CONV_WIDTH = 3
EPS = 1e-6
IN_WIDTH = 3 * ATTN_WIDTH + POOL_WIDTH + 2 * D_MODEL

LANES = 128
POOL_HALO = 16
CONV_HALO = 8
FF_CHUNK = 256
VMEM_LIMIT = 56 * 1024 * 1024
LOG2E = 1.4426950408889634
DEAD_LOG2 = 160.0
SOFTPLUS_CLAMP = 126.0

F32 = jnp.float32
BF16 = jnp.bfloat16


def _rmsnorm(x, g):
    return x * lax.rsqrt(jnp.mean(x * x, axis=-1, keepdims=True) + EPS) * g


def _const_spec(shape):
    nd = len(shape)
    return pl.BlockSpec(shape, lambda *_: (0,) * nd, pipeline_mode=pl.Buffered(1))


def _inproj_kernel(x_ref, g_ref, w_ref, q_ref, k_ref, v_ref, u_ref, kb_ref, vb_ref, ga_ref, gb_ref,
                   *, kv_transposed):
    h = _rmsnorm(x_ref[...], g_ref[...]).astype(BF16)

    def proj(lo, hi):
        return jnp.dot(h, w_ref[:, lo:hi].astype(BF16), preferred_element_type=F32)

    a = ATTN_WIDTH
    u0 = 3 * a
    g0 = u0 + POOL_WIDTH
    ga_ref[...] = jax.nn.sigmoid(proj(g0, g0 + D_MODEL)).astype(BF16)
    gb_ref[...] = jax.nn.sigmoid(proj(g0 + D_MODEL, g0 + 2 * D_MODEL)).astype(BF16)
    k = proj(a, 2 * a)
    kb_ref[...] = k.astype(BF16)
    v = proj(2 * a, 3 * a)
    vb_ref[...] = v.astype(BF16)
    if kv_transposed:
        k_ref[0] = k.T
        v_ref[0] = v.T
    else:
        k_ref[...] = k
        v_ref[...] = v
    q_ref[...] = (proj(0, a) * (HEAD_DIM ** -0.5 * LOG2E)).astype(BF16)
    u_ref[...] = proj(u0, u0 + POOL_WIDTH)


def _inproj(x2d, norm_mix, w_in, tm, seq_len, kv_transposed):
    m = x2d.shape[0]
    row = lambda w: pl.BlockSpec((tm, w), lambda i: (i, 0))
    outs = [(ATTN_WIDTH, BF16), (ATTN_WIDTH, F32), (ATTN_WIDTH, F32), (POOL_WIDTH, F32),
            (ATTN_WIDTH, BF16), (ATTN_WIDTH, BF16), (D_MODEL, BF16), (D_MODEL, BF16)]
    out_specs = [row(w) for w, _ in outs]
    out_shape = [jax.ShapeDtypeStruct((m, w), dt) for w, dt in outs]
    if kv_transposed:
        tiles = seq_len // tm
        for i in (1, 2):
            out_specs[i] = pl.BlockSpec((1, ATTN_WIDTH, tm), lambda i: (i // tiles, 0, i % tiles))
            out_shape[i] = jax.ShapeDtypeStruct((m // seq_len, ATTN_WIDTH, seq_len), F32)
    return pl.pallas_call(
        functools.partial(_inproj_kernel, kv_transposed=kv_transposed),
        grid=(m // tm,),
        in_specs=[row(D_MODEL), _const_spec((1, D_MODEL)), _const_spec((D_MODEL, IN_WIDTH))],
        out_specs=out_specs,
        out_shape=out_shape,
        compiler_params=pltpu.CompilerParams(dimension_semantics=("arbitrary",),
                                             vmem_limit_bytes=VMEM_LIMIT),
        name="inproj",
    )(x2d, norm_mix.reshape(1, D_MODEL), w_in)


_NT = (((1,), (1,)), ((), ()))


def _softplus2(z):
    return jnp.maximum(jnp.log2(1.0 + jnp.exp2(jnp.minimum(z, SOFTPLUS_CLAMP))), z)


def _scores(qrows, kblk, k_transposed):
    if k_transposed:
        return jnp.dot(qrows, kblk, preferred_element_type=F32)
    return lax.dot_general(qrows, kblk, _NT, preferred_element_type=F32)


def _fail_bits(z, mask):
    sp = _softplus2(z)
    return sp if mask is None else jnp.where(mask, sp, 0.0)


def _upto(sp, tri):
    return jnp.dot(sp.astype(BF16), tri, preferred_element_type=F32)


def _weights(z, upto, carry, mask):
    a = jnp.exp2(z - upto - carry)
    if mask is not None:
        a = jnp.where(mask, a, 0.0)
    return a.astype(BF16)


def _apply(a, vblk, v_transposed):
    if v_transposed:
        return lax.dot_general(a, vblk, _NT, preferred_element_type=F32)
    return jnp.dot(a, vblk, preferred_element_type=F32)


def _stick_scores(qrows, kblk, tri, mask, k_transposed):
    z = _scores(qrows, kblk, k_transposed)
    sp = _fail_bits(z, mask)
    return z, sp, _upto(sp, tri)


def _stick_values(z, upto, carry, vblk, mask, v_transposed):
    return _apply(_weights(z, upto, carry, mask), vblk, v_transposed)


def _row_sum(sp):
    return jnp.sum(sp, axis=1, keepdims=True)


N_CHAIN_STAGES = 5


def _chain(q, kblk, vblk, tri, mask, carry_of, transposed):
    z = _scores(q, kblk, transposed)
    yield None
    sp = _fail_bits(z, mask)
    yield _row_sum(sp)
    upto = _upto(sp, tri)
    yield None
    a = _weights(z, upto, carry_of(), mask)
    yield None
    yield _apply(a, vblk, transposed)


def _wavefront(chains):
    for step in range(len(chains) + N_CHAIN_STAGES - 1):
        for i, (stages, got) in enumerate(chains):
            if 0 <= step - i < N_CHAIN_STAGES:
                got.append(next(stages))


def _attn_prompt_kernel(q_ref, k_ref, v_ref, tri_ref, o_ref, *, blk):
    n_q = q_ref.shape[1] // blk
    n_tiles = q_ref.shape[2] // LANES
    rows = 2 * blk
    tri = tri_ref[...]
    lower = lax.broadcasted_iota(jnp.int32, (blk, LANES), 1) < HEAD_DIM
    r = lax.broadcasted_iota(jnp.int32, (rows, blk), 0)
    c = lax.broadcasted_iota(jnp.int32, (rows, blk), 1)
    diag_mask = c < (r & (blk - 1))
    no_carry = jnp.zeros((rows, 1), F32)

    def q_rows(b, t):
        q2 = q_ref[0, b * blk:(b + 1) * blk, t * LANES:(t + 1) * LANES]
        zero = jnp.zeros_like(q2)
        return jnp.concatenate([jnp.where(lower, q2, zero), jnp.where(lower, zero, q2)], axis=0)

    def kv(j, t):
        start = pl.multiple_of(j * blk, blk)
        lanes = slice(t * LANES, (t + 1) * LANES)
        return k_ref[0, pl.ds(start, blk), lanes], v_ref[0, pl.ds(start, blk), lanes]

    def newest_two():
        qis = [pl.program_id(1) * n_q + b for b in range(n_q)]
        qs, chains = [], []
        for b in range(n_q):
            for t in range(n_tiles):
                q = q_rows(b, t)
                kd, vd = kv(qis[b], t)
                kp, vp = kv(jnp.maximum(qis[b] - 1, 0), t)
                diag = (_chain(q, kd, vd, tri, diag_mask, lambda: no_carry, False), [])
                prev = (_chain(q, kp, vp, tri, None, lambda got=diag[1]: got[1], False), [])
                qs.append(q)
                chains += [diag, prev]
        _wavefront(chains)
        started = []
        for b in range(n_q):
            has_prev = qis[b] > 0
            accs, carries = [], []
            for t in range(n_tiles):
                diag, prev = (got for _, got in chains[2 * (b * n_tiles + t):][:2])
                accs.append(diag[-1] + jnp.where(has_prev, prev[-1], 0.0))
                carries.append(diag[1] + jnp.where(has_prev, prev[1], 0.0))
            started.append((qis[b], qs[b * n_tiles:(b + 1) * n_tiles], accs, carries))
        return started

    def live(carries):
        return jnp.min(functools.reduce(jnp.minimum, carries)) < DEAD_LOG2

    def older(qi, qs, accs, carries):
        def cond(state):
            j, _, _, alive = state
            return jnp.logical_and(j >= 0, alive)

        def body(state):
            j, accs, carries, _ = state
            new_accs, new_carries = [], []
            for t in range(n_tiles):
                kblk, vblk = kv(j, t)
                z, sp, upto = _stick_scores(qs[t], kblk, tri, None, False)
                new_accs.append(accs[t] + _stick_values(z, upto, carries[t], vblk, None, False))
                new_carries.append(carries[t] + _row_sum(sp))
            return j - 1, tuple(new_accs), tuple(new_carries), live(new_carries)

        return lax.while_loop(cond, body, (qi - 2, tuple(accs), tuple(carries), True))[1]

    def write(b, accs):
        for t in range(n_tiles):
            o_ref[0, b * blk:(b + 1) * blk, t * LANES:(t + 1) * LANES] = jnp.where(
                lower, accs[t][:blk], accs[t][blk:]).astype(o_ref.dtype)

    for b, (qi, qs, accs, carries) in enumerate(newest_two()):
        write(b, accs)

        @pl.when(jnp.logical_and(qi >= 2, live(carries)))
        def _():
            write(b, older(qi, qs, accs, carries))


def _tri(blk):
    j = lax.broadcasted_iota(jnp.int32, (blk, blk), 0)
    s = lax.broadcasted_iota(jnp.int32, (blk, blk), 1)
    return (j >= s).astype(BF16)


def _attn_prompt(q, kb, vb, blk, q_blocks):
    b, t, _ = q.shape
    qspec = pl.BlockSpec((1, q_blocks * blk, ATTN_WIDTH), lambda bi, qi: (bi, qi, 0))
    kvspec = pl.BlockSpec((1, t, ATTN_WIDTH), lambda bi, qi: (bi, 0, 0))
    return pl.pallas_call(
        functools.partial(_attn_prompt_kernel, blk=blk),
        grid=(b, t // (q_blocks * blk)),
        in_specs=[qspec, kvspec, kvspec, _const_spec((blk, blk))],
        out_specs=qspec,
        out_shape=jax.ShapeDtypeStruct((b, t, ATTN_WIDTH), BF16),
        compiler_params=pltpu.CompilerParams(
            dimension_semantics=("arbitrary", "arbitrary"),
            vmem_limit_bytes=VMEM_LIMIT),
        name="attn_prompt",
    )(q, kb, vb, _tri(blk))


def _attn_sample_kernel(q_ref, kn_ref, vn_ref, ck_ref, cv_ref, tri_ref, o_ref, carry_ref, *, blk, new_blk):
    n_seq, tq, _ = q_ref.shape
    rows = N_HEADS * tq
    r = lax.broadcasted_iota(jnp.int32, (rows, ATTN_WIDTH), 0)
    c = lax.broadcasted_iota(jnp.int32, (rows, ATTN_WIDTH), 1)
    tq_bits = tq.bit_length() - 1
    head_bits = HEAD_DIM.bit_length() - 1
    assert tq == 1 << tq_bits
    head_sel = (r >> tq_bits) == (c >> head_bits)
    tri = tri_ref[...]
    tri_new = tri[:new_blk, :new_blk]
    rn = lax.broadcasted_iota(jnp.int32, (rows, new_blk), 0)
    cn = lax.broadcasted_iota(jnp.int32, (rows, new_blk), 1)
    new_mask = cn < (rn & (tq - 1))
    pad = jnp.zeros((new_blk - tq, ATTN_WIDTH), BF16)
    no_carry = jnp.zeros((rows, 1), F32)
    co = lax.broadcasted_iota(jnp.int32, (tq, ATTN_WIDTH), 1) >> head_bits
    window = ck_ref.shape[2]

    def carry_after(earlier):
        return lambda: functools.reduce(jnp.add, [got[1] for got in earlier], no_carry)

    per_seq, chains = [], []
    for s in range(n_seq):
        q = q_ref[s]
        qrep = jnp.concatenate([q] * N_HEADS, axis=0)
        qrows = jnp.where(head_sel, qrep, jnp.zeros_like(qrep))
        mine = [(_chain(qrows, jnp.concatenate([kn_ref[s], pad], axis=0),
                        jnp.concatenate([vn_ref[s], pad], axis=0), tri_new, new_mask,
                        lambda: no_carry, False), [])]
        for j in reversed(range(window // blk)):
            kblk = ck_ref[s, :, j * blk:(j + 1) * blk].astype(BF16)
            vblk = cv_ref[s, :, j * blk:(j + 1) * blk].astype(BF16)
            mine.append((_chain(qrows, kblk, vblk, tri, None,
                                carry_after([got for _, got in mine]), True), []))
        per_seq.append(mine)
        chains += mine
    _wavefront(chains)

    for s, mine in enumerate(per_seq):
        parts = [got[-1] for _, got in mine]
        while len(parts) > 1:
            parts = [functools.reduce(jnp.add, parts[i:i + 2]) for i in range(0, len(parts), 2)]
        acc = parts[0]
        out = jnp.zeros((tq, ATTN_WIDTH), F32)
        for h in range(N_HEADS):
            out = jnp.where(co == h, acc[h * tq:(h + 1) * tq], out)
        o_ref[s] = out.astype(o_ref.dtype)
        carry = carry_after([got for _, got in mine])()
        carry_ref[s] = jnp.broadcast_to(jnp.min(carry, axis=0, keepdims=True), carry_ref.shape[1:])


def _attn_sample_window(q, kb, vb, cache_k, cache_v, blk, new_blk, window, n_seq):
    b, tq, _ = q.shape
    past = cache_k.shape[2]
    new = pl.BlockSpec((n_seq, tq, ATTN_WIDTH), lambda bi: (bi, 0, 0))
    old = pl.BlockSpec((n_seq, ATTN_WIDTH, window), lambda bi: (bi, 0, past // window - 1))
    low = pl.BlockSpec((n_seq, 8, LANES), lambda bi: (bi, 0, 0))
    return pl.pallas_call(
        functools.partial(_attn_sample_kernel, blk=blk, new_blk=new_blk),
        grid=(b // n_seq,),
        in_specs=[new, new, new, old, old, _const_spec((blk, blk))],
        out_specs=[new, low],
        out_shape=[jax.ShapeDtypeStruct((b, tq, ATTN_WIDTH), BF16),
                   jax.ShapeDtypeStruct((b, 8, LANES), F32)],
        compiler_params=pltpu.CompilerParams(dimension_semantics=("arbitrary",),
                                             vmem_limit_bytes=VMEM_LIMIT),
        name="attn_sample",
    )(q, kb, vb, cache_k, cache_v, _tri(blk))


def _attn_sample(q, kb, vb, cache_k, cache_v, blk, new_blk):
    past = cache_k.shape[2]
    recent, min_carry = _attn_sample_window(q, kb, vb, cache_k, cache_v, blk, new_blk, blk, 4)
    if past == blk:
        return recent
    return lax.cond(jnp.min(min_carry) < DEAD_LOG2,
                    lambda: _attn_sample_window(q, kb, vb, cache_k, cache_v, blk, new_blk, past, 2)[0],
                    lambda: recent)


def _with_history(hist, cur):
    nb, h, c = hist.shape
    t = cur.shape[1]
    return jnp.concatenate([hist, cur], axis=1).reshape(nb * (h + t), c)


def _drop_history(flat, nb, h, t):
    c = flat.shape[-1]
    return flat.reshape(nb, h + t, c)[:, h:, :].reshape(nb * t, c)


def _twice_gelu_tanh(x):
    c = 0.7978845608028654
    return x * (1.0 + jnp.tanh(x * (c + (c * 0.044715) * (x * x))))


def _post_kernel(x_ref, attn_ref, u_ref, ga_ref, gb_ref, *rest, pos0, has_state):
    state_refs, rest = (rest[:2], rest[2:]) if has_state else ((), rest)
    (wa_ref, wb_ref, pw_ref, ps_ref, wo_ref, nffn_ref, wup_ref, cw_ref, cb_ref, wdn_ref, nfin_ref,
     y_ref, ptail_ref, ctail_ref, uhist_ref, chist_ref, h2_ref, act_ref) = rest
    nb, tt, _ = x_ref.shape
    m = nb * tt
    ti = pl.program_id(1)

    @pl.when(ti == 0)
    def _():
        uhist_ref[...] = jnp.zeros(uhist_ref.shape, F32)
        chist_ref[...] = jnp.zeros(chist_ref.shape, F32)
        if has_state:
            uhist_ref[:, POOL_HALO - POOL_STATE:, :] = state_refs[0][...]
            chist_ref[:, CONV_HALO - (CONV_WIDTH - 1):, :] = state_refs[1][...]

    branch_a = jnp.dot(attn_ref[...].reshape(m, ATTN_WIDTH), wa_ref[...].astype(BF16), preferred_element_type=F32)
    x = x_ref[...].reshape(m, D_MODEL)
    u3 = u_ref[...]
    u = u3.reshape(m, POOL_WIDTH)

    ext = _with_history(uhist_ref[...], u3)
    row = lax.broadcasted_iota(jnp.int32, (m, POOL_GROUP), 0)
    pos = pos0 + ti * tt + (row & (tt - 1))
    pmix = []
    for g, w in enumerate(POOL_WINDOWS):
        sl = slice(g * POOL_GROUP, (g + 1) * POOL_GROUP)
        s = ext[:, sl]
        shift = 1
        while shift < w:
            s = s + pltpu.roll(s, shift, axis=0)
            shift *= 2
        win = _drop_history(s, nb, POOL_HALO, tt)
        cnt = jnp.minimum(pos + 1, w).astype(F32)
        pooled = win / cnt - u[:, sl]
        pmix.append(jnp.dot(pooled.astype(BF16), pw_ref[g].astype(BF16), preferred_element_type=F32))
    pmix = jnp.concatenate(pmix, axis=1) * ps_ref[...]
    branch_b = jnp.dot(pmix.astype(BF16), wb_ref[...].astype(BF16), preferred_element_type=F32)
    merged = (ga_ref[...].reshape(m, D_MODEL).astype(F32) * branch_a
              + gb_ref[...].reshape(m, D_MODEL).astype(F32) * branch_b)
    x1 = x + jnp.dot(merged.astype(BF16), wo_ref[...].astype(BF16), preferred_element_type=F32)

    tail = u3[:, tt - POOL_HALO:, :]
    uhist_ref[...] = tail
    ptail_ref[...] = tail[:, POOL_HALO - POOL_STATE:, :]

    y_ref[...] = x1.reshape(nb, tt, D_MODEL)
    h2_ref[...] = _rmsnorm(x1, nffn_ref[...]).astype(BF16)
    n_chunks = D_FF // FF_CHUNK

    def up_proj(j):
        cols = [slice(half * D_FF + j * FF_CHUNK, half * D_FF + (j + 1) * FF_CHUNK) for half in range(2)]
        return [(cs, jnp.dot(h2_ref[...], wup_ref[:, cs], preferred_element_type=F32)) for cs in cols]

    ups = up_proj(0)
    for j in range(n_chunks):
        cur, ups = ups, (up_proj(j + 1) if j + 1 < n_chunks else None)
        halves = []
        for cs, up in cur:
            up3 = up.reshape(nb, tt, FF_CHUNK)
            e = _with_history(chist_ref[:, :, cs], up3)
            conv = cb_ref[:, cs] + up * cw_ref[2:3, cs]
            for d in (1, 2):
                shifted = _drop_history(pltpu.roll(e, d, axis=0), nb, CONV_HALO, tt)
                conv = conv + shifted * cw_ref[2 - d:3 - d, cs]
            halves.append(conv)
            ctile = up3[:, tt - CONV_HALO:, :]
            chist_ref[:, :, cs] = ctile
            ctail_ref[:, :, cs] = ctile[:, CONV_HALO - (CONV_WIDTH - 1):, :]
        act = (_twice_gelu_tanh(halves[0]) * halves[1]).astype(BF16)
        act_ref[:, j * FF_CHUNK:(j + 1) * FF_CHUNK] = act
    down = jnp.dot(act_ref[...], wdn_ref[...], preferred_element_type=F32)
    y = _rmsnorm(y_ref[...].reshape(m, D_MODEL) + down, nfin_ref[...])
    y_ref[...] = y.reshape(nb, tt, D_MODEL)


def _post(x, attn, u, ga, gb, state, weights, nb, tt, pos0):
    b, t, _ = x.shape
    (w_a, w_b, pool_w, pool_scale, w_o, norm_ffn, w_up, conv_w, conv_b, w_down, norm_final) = weights
    tile = lambda w: pl.BlockSpec((nb, tt, w), lambda bi, ti: (bi, ti, 0))
    per_seq = lambda r, w: pl.BlockSpec((nb, r, w), lambda bi, ti: (bi, 0, 0))
    consts = [w_a, w_b, pool_w, pool_scale, w_o, norm_ffn, w_up, conv_w, conv_b, w_down, norm_final]
    return pl.pallas_call(
        functools.partial(_post_kernel, pos0=pos0, has_state=state is not None),
        grid=(b // nb, t // tt),
        in_specs=[tile(D_MODEL), tile(ATTN_WIDTH), tile(POOL_WIDTH), tile(D_MODEL), tile(D_MODEL)]
                 + ([per_seq(POOL_STATE, POOL_WIDTH), per_seq(CONV_WIDTH - 1, 2 * D_FF)] if state else [])
                 + [_const_spec(c.shape) for c in consts],
        out_specs=[tile(D_MODEL), per_seq(POOL_STATE, POOL_WIDTH), per_seq(CONV_WIDTH - 1, 2 * D_FF)],
        out_shape=[jax.ShapeDtypeStruct((b, t, D_MODEL), F32),
                   jax.ShapeDtypeStruct((b, POOL_STATE, POOL_WIDTH), F32),
                   jax.ShapeDtypeStruct((b, CONV_WIDTH - 1, 2 * D_FF), F32)],
        scratch_shapes=[pltpu.VMEM((nb, POOL_HALO, POOL_WIDTH), F32),
                        pltpu.VMEM((nb, CONV_HALO, 2 * D_FF), F32),
                        pltpu.VMEM((nb * tt, D_MODEL), BF16),
                        pltpu.VMEM((nb * tt, D_FF), BF16)],
        compiler_params=pltpu.CompilerParams(dimension_semantics=("arbitrary", "arbitrary"),
                                             vmem_limit_bytes=VMEM_LIMIT),
        name="post",
    )(x, attn, u, ga, gb, *(state or ()), *consts)


def _stream(x, past, state, norm_mix, w_in, post_weights, *, tm, nb, tt, pos0):
    b, t, _ = x.shape
    m = b * t
    kv_transposed = past is None
    q, k, v, u, kb, vb, ga, gb = _inproj(x.reshape(m, D_MODEL), norm_mix, w_in, tm, t, kv_transposed)
    seq = lambda a: a.reshape(b, t, a.shape[-1])
    if past is None:
        attn = _attn_prompt(seq(q), seq(kb), seq(vb), blk=256, q_blocks=4)
        heads = lambda a: a.reshape(1, b, N_HEADS, HEAD_DIM, t).transpose(0, 1, 4, 2, 3)
    else:
        time_last = lambda c: c.transpose(0, 2, 3, 1).reshape(b, ATTN_WIDTH, c.shape[1])
        attn = _attn_sample(seq(q), seq(kb), seq(vb), time_last(past[0]), time_last(past[1]),
                            blk=256, new_blk=128)
        heads = lambda a: a.reshape(1, b, t, N_HEADS, HEAD_DIM)
    y, ptail, ctail = _post(x, attn, seq(u), seq(ga), seq(gb), state, post_weights, nb, tt, pos0)
    return (y, heads(k), heads(v),
            ptail[None], ctail[None])


def kernel(x_prompt, x_sample, cache_k, cache_v, state_pool, state_conv, norm_mix, w_in, w_a, w_b,
           pool_w, pool_scale, w_o, norm_ffn, w_up, conv_w, conv_b, w_down, norm_final):
    assert w_in.shape[0] == 1, "single-layer trunk"
    row = lambda a: a.reshape(1, -1)
    post_weights = (w_a[0], w_b[0], pool_w[0], row(pool_scale[0]), w_o[0], row(norm_ffn[0]),
                    w_up[0].astype(BF16), conv_w[0], row(conv_b[0]), (0.5 * w_down[0]).astype(BF16),
                    row(norm_final))
    yp, kp, vp, pp, cp = _stream(x_prompt, None, None, norm_mix[0], w_in[0],
                                 post_weights, tm=1024, nb=1, tt=512, pos0=0)
    bs, ts, _ = x_sample.shape
    ys, ks, vs, ps, cs = _stream(x_sample, (cache_k[0], cache_v[0]), (state_pool[0], state_conv[0]),
                                 norm_mix[0], w_in[0], post_weights,
                                 tm=bs * ts, nb=bs, tt=ts, pos0=cache_k.shape[2])
    return (yp, ys, kp, vp, pp, cp, ks, vs, ps, cs)
```

```python
import functools

import jax
import jax.numpy as jnp
from jax import lax
from jax.experimental import pallas as pl
from jax.experimental.pallas import tpu as pltpu

D_MODEL = 1024
N_HEADS = 8
HEAD_DIM = 64
ATTN_WIDTH = N_HEADS * HEAD_DIM
POOL_WINDOWS = (2, 4, 8, 16)
POOL_WIDTH = 512
POOL_GROUP = POOL_WIDTH // len(POOL_WINDOWS)
POOL_STATE = max(POOL_WINDOWS) - 1
D_FF = 2816
CONV_WIDTH = 3
EPS = 1e-6
IN_WIDTH = 3 * ATTN_WIDTH + POOL_WIDTH + 2 * D_MODEL

LANES = 128
BF16_SUBLANES = 16
POOL_HALO = 16
CONV_HALO = 8
FF_CHUNK = 256
VMEM_LIMIT = 56 * 1024 * 1024
LOG2E = 1.4426950408889634
DEAD_LOG2 = 160.0
SOFTPLUS_CLAMP = 126.0

F32 = jnp.float32
BF16 = jnp.bfloat16


def _rmsnorm(x, g):
    return x * lax.rsqrt(jnp.mean(x * x, axis=-1, keepdims=True) + EPS) * g


def _const_spec(shape):
    nd = len(shape)
    return pl.BlockSpec(shape, lambda *_: (0,) * nd, pipeline_mode=pl.Buffered(1))


def _inproj_kernel(x_ref, g_ref, w_ref, q_ref, k_ref, v_ref, u_ref, kb_ref, vb_ref, ga_ref, gb_ref,
                   *, kv_transposed):
    h = _rmsnorm(x_ref[...], g_ref[...]).astype(BF16)

    def proj(lo, hi):
        return jnp.dot(h, w_ref[:, lo:hi].astype(BF16), preferred_element_type=F32)

    a = ATTN_WIDTH
    u0 = 3 * a
    g0 = u0 + POOL_WIDTH
    ga_ref[...] = jax.nn.sigmoid(proj(g0, g0 + D_MODEL)).astype(BF16)
    gb_ref[...] = jax.nn.sigmoid(proj(g0 + D_MODEL, g0 + 2 * D_MODEL)).astype(BF16)
    k = proj(a, 2 * a)
    kb_ref[...] = k.astype(BF16)
    v = proj(2 * a, 3 * a)
    vb_ref[...] = v.astype(BF16)
    if kv_transposed:
        k_ref[0] = k.T
        v_ref[0] = v.T
    else:
        k_ref[...] = k
        v_ref[...] = v
    q_ref[...] = (proj(0, a) * (HEAD_DIM ** -0.5 * LOG2E)).astype(BF16)
    u_ref[...] = proj(u0, u0 + POOL_WIDTH)


def _inproj(x2d, norm_mix, w_in, tm, seq_len, kv_transposed):
    m = x2d.shape[0]
    row = lambda w: pl.BlockSpec((tm, w), lambda i: (i, 0))
    outs = [(ATTN_WIDTH, BF16), (ATTN_WIDTH, F32), (ATTN_WIDTH, F32), (POOL_WIDTH, F32),
            (ATTN_WIDTH, BF16), (ATTN_WIDTH, BF16), (D_MODEL, BF16), (D_MODEL, BF16)]
    out_specs = [row(w) for w, _ in outs]
    out_shape = [jax.ShapeDtypeStruct((m, w), dt) for w, dt in outs]
    if kv_transposed:
        tiles = seq_len // tm
        for i in (1, 2):
            out_specs[i] = pl.BlockSpec((1, ATTN_WIDTH, tm), lambda i: (i // tiles, 0, i % tiles))
            out_shape[i] = jax.ShapeDtypeStruct((m // seq_len, ATTN_WIDTH, seq_len), F32)
    return pl.pallas_call(
        functools.partial(_inproj_kernel, kv_transposed=kv_transposed),
        grid=(m // tm,),
        in_specs=[row(D_MODEL), _const_spec((1, D_MODEL)), _const_spec((D_MODEL, IN_WIDTH))],
        out_specs=out_specs,
        out_shape=out_shape,
        compiler_params=pltpu.CompilerParams(dimension_semantics=("arbitrary",),
                                             vmem_limit_bytes=VMEM_LIMIT),
        name="inproj",
    )(x2d, norm_mix.reshape(1, D_MODEL), w_in)


_NT = (((1,), (1,)), ((), ()))


def _softplus2(z):
    return jnp.maximum(jnp.log2(1.0 + jnp.exp2(jnp.minimum(z, SOFTPLUS_CLAMP))), z)


def _scores(qrows, kblk, k_transposed):
    if k_transposed:
        return jnp.dot(qrows, kblk, preferred_element_type=F32)
    return lax.dot_general(qrows, kblk, _NT, preferred_element_type=F32)


def _fail_bits(z, mask):
    sp = _softplus2(z)
    return sp if mask is None else jnp.where(mask, sp, 0.0)


def _upto(sp, tri):
    return jnp.dot(sp.astype(BF16), tri, preferred_element_type=F32)


def _weights(z, upto, carry, mask):
    a = jnp.exp2(z - upto - carry)
    if mask is not None:
        a = jnp.where(mask, a, 0.0)
    return a.astype(BF16)


def _apply(a, vblk, v_transposed):
    if v_transposed:
        return lax.dot_general(a, vblk, _NT, preferred_element_type=F32)
    return jnp.dot(a, vblk, preferred_element_type=F32)


def _stick_scores(qrows, kblk, tri, mask, k_transposed):
    z = _scores(qrows, kblk, k_transposed)
    sp = _fail_bits(z, mask)
    return z, sp, _upto(sp, tri)


def _stick_values(z, upto, carry, vblk, mask, v_transposed):
    return _apply(_weights(z, upto, carry, mask), vblk, v_transposed)


def _row_sum(sp):
    return jnp.sum(sp, axis=1, keepdims=True)


N_CHAIN_STAGES = 5


def _chain(q, kblk, vblk, tri, mask, carry_of, transposed):
    z = _scores(q, kblk, transposed)
    yield None
    sp = _fail_bits(z, mask)
    yield _row_sum(sp)
    upto = _upto(sp, tri)
    yield None
    a = _weights(z, upto, carry_of(), mask)
    yield None
    yield _apply(a, vblk, transposed)


def _wavefront(chains):
    for step in range(len(chains) + N_CHAIN_STAGES - 1):
        for i, (stages, got) in enumerate(chains):
            if 0 <= step - i < N_CHAIN_STAGES:
                got.append(next(stages))


def _attn_prompt_kernel(q_ref, k_ref, v_ref, tri_ref, *rest, blk, side_scales):
    n_side = len(side_scales)
    wide_refs, o_ref, narrow_refs = rest[:n_side], rest[n_side], rest[n_side + 1:]
    for wide_ref, narrow_ref, scale in zip(wide_refs, narrow_refs, side_scales):
        narrow_ref[...] = (wide_ref[...] * scale).astype(BF16)

    n_q = q_ref.shape[1] // blk
    n_tiles = q_ref.shape[2] // LANES
    rows = 2 * blk
    tri = tri_ref[...]
    lower = lax.broadcasted_iota(jnp.int32, (blk, LANES), 1) < HEAD_DIM
    r = lax.broadcasted_iota(jnp.int32, (rows, blk), 0)
    c = lax.broadcasted_iota(jnp.int32, (rows, blk), 1)
    diag_mask = c < (r & (blk - 1))
    no_carry = jnp.zeros((rows, 1), F32)

    def q_rows(b, t):
        q2 = q_ref[0, b * blk:(b + 1) * blk, t * LANES:(t + 1) * LANES]
        zero = jnp.zeros_like(q2)
        return jnp.concatenate([jnp.where(lower, q2, zero), jnp.where(lower, zero, q2)], axis=0)

    def kv(j, t):
        start = pl.multiple_of(j * blk, blk)
        lanes = slice(t * LANES, (t + 1) * LANES)
        return k_ref[0, pl.ds(start, blk), lanes], v_ref[0, pl.ds(start, blk), lanes]

    def newest_two():
        qis = [pl.program_id(1) * n_q + b for b in range(n_q)]
        qs, chains = [], []
        for b in range(n_q):
            for t in range(n_tiles):
                q = q_rows(b, t)
                kd, vd = kv(qis[b], t)
                kp, vp = kv(jnp.maximum(qis[b] - 1, 0), t)
                diag = (_chain(q, kd, vd, tri, diag_mask, lambda: no_carry, False), [])
                prev = (_chain(q, kp, vp, tri, None, lambda got=diag[1]: got[1], False), [])
                qs.append(q)
                chains += [diag, prev]
        _wavefront(chains)
        started = []
        for b in range(n_q):
            has_prev = qis[b] > 0
            accs, carries = [], []
            for t in range(n_tiles):
                diag, prev = (got for _, got in chains[2 * (b * n_tiles + t):][:2])
                accs.append(diag[-1] + jnp.where(has_prev, prev[-1], 0.0))
                carries.append(diag[1] + jnp.where(has_prev, prev[1], 0.0))
            started.append((qis[b], qs[b * n_tiles:(b + 1) * n_tiles], accs, carries))
        return started

    def live(carries):
        return jnp.min(functools.reduce(jnp.minimum, carries)) < DEAD_LOG2

    def older(qi, qs, accs, carries):
        def cond(state):
            j, _, _, alive = state
            return jnp.logical_and(j >= 0, alive)

        def body(state):
            j, accs, carries, _ = state
            new_accs, new_carries = [], []
            for t in range(n_tiles):
                kblk, vblk = kv(j, t)
                z, sp, upto = _stick_scores(qs[t], kblk, tri, None, False)
                new_accs.append(accs[t] + _stick_values(z, upto, carries[t], vblk, None, False))
                new_carries.append(carries[t] + _row_sum(sp))
            return j - 1, tuple(new_accs), tuple(new_carries), live(new_carries)

        return lax.while_loop(cond, body, (qi - 2, tuple(accs), tuple(carries), True))[1]

    def write(b, accs):
        for t in range(n_tiles):
            o_ref[0, b * blk:(b + 1) * blk, t * LANES:(t + 1) * LANES] = jnp.where(
                lower, accs[t][:blk], accs[t][blk:]).astype(o_ref.dtype)

    for b, (qi, qs, accs, carries) in enumerate(newest_two()):
        write(b, accs)

        @pl.when(jnp.logical_and(qi >= 2, live(carries)))
        def _():
            write(b, older(qi, qs, accs, carries))


def _tri(blk):
    j = lax.broadcasted_iota(jnp.int32, (blk, blk), 0)
    s = lax.broadcasted_iota(jnp.int32, (blk, blk), 1)
    return (j >= s).astype(BF16)


def _slab_rows(n_rows, n_steps):
    return next(r for r in range(BF16_SUBLANES, n_rows + 1, BF16_SUBLANES)
                if n_rows % r == 0 and n_rows // r <= n_steps)


def _attn_prompt(q, kb, vb, blk, q_blocks, side):
    b, t, _ = q.shape
    steps_q = t // (q_blocks * blk)
    qspec = pl.BlockSpec((1, q_blocks * blk, ATTN_WIDTH), lambda bi, qi: (bi, qi, 0))
    kvspec = pl.BlockSpec((1, t, ATTN_WIDTH), lambda bi, qi: (bi, 0, 0))

    def slab(w):
        rows = _slab_rows(w.shape[0], b * steps_q)
        last = w.shape[0] // rows - 1
        return pl.BlockSpec((rows, w.shape[1]), lambda bi, qi: (jnp.minimum(bi * steps_q + qi, last), 0))

    slabs = [slab(w) for w, _ in side]
    attn, *narrow = pl.pallas_call(
        functools.partial(_attn_prompt_kernel, blk=blk, side_scales=tuple(scale for _, scale in side)),
        grid=(b, steps_q),
        in_specs=[qspec, kvspec, kvspec, _const_spec((blk, blk))] + slabs,
        out_specs=[qspec] + slabs,
        out_shape=[jax.ShapeDtypeStruct((b, t, ATTN_WIDTH), BF16)]
                  + [jax.ShapeDtypeStruct(w.shape, BF16) for w, _ in side],
        compiler_params=pltpu.CompilerParams(
            dimension_semantics=("arbitrary", "arbitrary"),
            vmem_limit_bytes=VMEM_LIMIT),
        name="attn_prompt",
    )(q, kb, vb, _tri(blk), *[w for w, _ in side])
    return attn, narrow


def _attn_sample_kernel(q_ref, kn_ref, vn_ref, ck_ref, cv_ref, tri_ref, o_ref, carry_ref, *, blk, new_blk):
    n_seq, tq, _ = q_ref.shape
    rows = N_HEADS * tq
    r = lax.broadcasted_iota(jnp.int32, (rows, ATTN_WIDTH), 0)
    c = lax.broadcasted_iota(jnp.int32, (rows, ATTN_WIDTH), 1)
    tq_bits = tq.bit_length() - 1
    head_bits = HEAD_DIM.bit_length() - 1
    assert tq == 1 << tq_bits
    head_sel = (r >> tq_bits) == (c >> head_bits)
    tri = tri_ref[...]
    tri_new = tri[:new_blk, :new_blk]
    rn = lax.broadcasted_iota(jnp.int32, (rows, new_blk), 0)
    cn = lax.broadcasted_iota(jnp.int32, (rows, new_blk), 1)
    new_mask = cn < (rn & (tq - 1))
    pad = jnp.zeros((new_blk - tq, ATTN_WIDTH), BF16)
    no_carry = jnp.zeros((rows, 1), F32)
    co = lax.broadcasted_iota(jnp.int32, (tq, ATTN_WIDTH), 1) >> head_bits
    window = ck_ref.shape[2]

    def carry_after(earlier):
        return lambda: functools.reduce(jnp.add, [got[1] for got in earlier], no_carry)

    per_seq, chains = [], []
    for s in range(n_seq):
        q = q_ref[s]
        qrep = jnp.concatenate([q] * N_HEADS, axis=0)
        qrows = jnp.where(head_sel, qrep, jnp.zeros_like(qrep))
        mine = [(_chain(qrows, jnp.concatenate([kn_ref[s], pad], axis=0),
                        jnp.concatenate([vn_ref[s], pad], axis=0), tri_new, new_mask,
                        lambda: no_carry, False), [])]
        for j in reversed(range(window // blk)):
            kblk = ck_ref[s, :, j * blk:(j + 1) * blk].astype(BF16)
            vblk = cv_ref[s, :, j * blk:(j + 1) * blk].astype(BF16)
            mine.append((_chain(qrows, kblk, vblk, tri, None,
                                carry_after([got for _, got in mine]), True), []))
        per_seq.append(mine)
        chains += mine
    _wavefront(chains)

    for s, mine in enumerate(per_seq):
        parts = [got[-1] for _, got in mine]
        while len(parts) > 1:
            parts = [functools.reduce(jnp.add, parts[i:i + 2]) for i in range(0, len(parts), 2)]
        acc = parts[0]
        out = jnp.zeros((tq, ATTN_WIDTH), F32)
        for h in range(N_HEADS):
            out = jnp.where(co == h, acc[h * tq:(h + 1) * tq], out)
        o_ref[s] = out.astype(o_ref.dtype)
        carry = carry_after([got for _, got in mine])()
        carry_ref[s] = jnp.broadcast_to(jnp.min(carry, axis=0, keepdims=True), carry_ref.shape[1:])


def _attn_sample_window(q, kb, vb, cache_k, cache_v, blk, new_blk, window, n_seq):
    b, tq, _ = q.shape
    past = cache_k.shape[2]
    new = pl.BlockSpec((n_seq, tq, ATTN_WIDTH), lambda bi: (bi, 0, 0))
    old = pl.BlockSpec((n_seq, ATTN_WIDTH, window), lambda bi: (bi, 0, past // window - 1))
    low = pl.BlockSpec((n_seq, 8, LANES), lambda bi: (bi, 0, 0))
    return pl.pallas_call(
        functools.partial(_attn_sample_kernel, blk=blk, new_blk=new_blk),
        grid=(b // n_seq,),
        in_specs=[new, new, new, old, old, _const_spec((blk, blk))],
        out_specs=[new, low],
        out_shape=[jax.ShapeDtypeStruct((b, tq, ATTN_WIDTH), BF16),
                   jax.ShapeDtypeStruct((b, 8, LANES), F32)],
        compiler_params=pltpu.CompilerParams(dimension_semantics=("arbitrary",),
                                             vmem_limit_bytes=VMEM_LIMIT),
        name="attn_sample",
    )(q, kb, vb, cache_k, cache_v, _tri(blk))


def _attn_sample(q, kb, vb, cache_k, cache_v, blk, new_blk):
    past = cache_k.shape[2]
    recent, min_carry = _attn_sample_window(q, kb, vb, cache_k, cache_v, blk, new_blk, blk, 4)
    if past == blk:
        return recent
    return lax.cond(jnp.min(min_carry) < DEAD_LOG2,
                    lambda: _attn_sample_window(q, kb, vb, cache_k, cache_v, blk, new_blk, past, 2)[0],
                    lambda: recent)


def _with_history(hist, cur):
    nb, h, c = hist.shape
    t = cur.shape[1]
    return jnp.concatenate([hist, cur], axis=1).reshape(nb * (h + t), c)


def _drop_history(flat, nb, h, t):
    c = flat.shape[-1]
    return flat.reshape(nb, h + t, c)[:, h:, :].reshape(nb * t, c)


def _twice_gelu_tanh(x):
    c = 0.7978845608028654
    return x * (1.0 + jnp.tanh(x * (c + (c * 0.044715) * (x * x))))


def _post_kernel(x_ref, attn_ref, u_ref, ga_ref, gb_ref, *rest, pos0, has_state):
    state_refs, rest = (rest[:2], rest[2:]) if has_state else ((), rest)
    (wa_ref, wb_ref, pw_ref, ps_ref, wo_ref, nffn_ref, wup_ref, cw_ref, cb_ref, wdn_ref, nfin_ref,
     y_ref, ptail_ref, ctail_ref, uhist_ref, chist_ref, h2_ref, act_ref) = rest
    nb, tt, _ = x_ref.shape
    m = nb * tt
    ti = pl.program_id(1)

    @pl.when(ti == 0)
    def _():
        uhist_ref[...] = jnp.zeros(uhist_ref.shape, F32)
        chist_ref[...] = jnp.zeros(chist_ref.shape, F32)
        if has_state:
            uhist_ref[:, POOL_HALO - POOL_STATE:, :] = state_refs[0][...]
            chist_ref[:, CONV_HALO - (CONV_WIDTH - 1):, :] = state_refs[1][...]

    branch_a = jnp.dot(attn_ref[...].reshape(m, ATTN_WIDTH), wa_ref[...].astype(BF16), preferred_element_type=F32)
    x = x_ref[...].reshape(m, D_MODEL)
    u3 = u_ref[...]
    u = u3.reshape(m, POOL_WIDTH)

    ext = _with_history(uhist_ref[...], u3)
    row = lax.broadcasted_iota(jnp.int32, (m, POOL_GROUP), 0)
    pos = pos0 + ti * tt + (row & (tt - 1))
    pmix = []
    for g, w in enumerate(POOL_WINDOWS):
        sl = slice(g * POOL_GROUP, (g + 1) * POOL_GROUP)
        s = ext[:, sl]
        shift = 1
        while shift < w:
            s = s + pltpu.roll(s, shift, axis=0)
            shift *= 2
        win = _drop_history(s, nb, POOL_HALO, tt)
        cnt = jnp.minimum(pos + 1, w).astype(F32)
        pooled = win / cnt - u[:, sl]
        pmix.append(jnp.dot(pooled.astype(BF16), pw_ref[g].astype(BF16), preferred_element_type=F32))
    pmix = jnp.concatenate(pmix, axis=1) * ps_ref[...]
    branch_b = jnp.dot(pmix.astype(BF16), wb_ref[...].astype(BF16), preferred_element_type=F32)
    merged = (ga_ref[...].reshape(m, D_MODEL).astype(F32) * branch_a
              + gb_ref[...].reshape(m, D_MODEL).astype(F32) * branch_b)
    x1 = x + jnp.dot(merged.astype(BF16), wo_ref[...].astype(BF16), preferred_element_type=F32)

    tail = u3[:, tt - POOL_HALO:, :]
    uhist_ref[...] = tail
    ptail_ref[...] = tail[:, POOL_HALO - POOL_STATE:, :]

    y_ref[...] = x1.reshape(nb, tt, D_MODEL)
    h2_ref[...] = _rmsnorm(x1, nffn_ref[...]).astype(BF16)
    n_chunks = D_FF // FF_CHUNK

    def up_proj(j):
        cols = [slice(half * D_FF + j * FF_CHUNK, half * D_FF + (j + 1) * FF_CHUNK) for half in range(2)]
        return [(cs, jnp.dot(h2_ref[...], wup_ref[:, cs], preferred_element_type=F32)) for cs in cols]

    ups = up_proj(0)
    for j in range(n_chunks):
        cur, ups = ups, (up_proj(j + 1) if j + 1 < n_chunks else None)
        halves = []
        for cs, up in cur:
            up3 = up.reshape(nb, tt, FF_CHUNK)
            e = _with_history(chist_ref[:, :, cs], up3)
            conv = cb_ref[:, cs] + up * cw_ref[2:3, cs]
            for d in (1, 2):
                shifted = _drop_history(pltpu.roll(e, d, axis=0), nb, CONV_HALO, tt)
                conv = conv + shifted * cw_ref[2 - d:3 - d, cs]
            halves.append(conv)
            ctile = up3[:, tt - CONV_HALO:, :]
            chist_ref[:, :, cs] = ctile
            ctail_ref[:, :, cs] = ctile[:, CONV_HALO - (CONV_WIDTH - 1):, :]
        act = (_twice_gelu_tanh(halves[0]) * halves[1]).astype(BF16)
        act_ref[:, j * FF_CHUNK:(j + 1) * FF_CHUNK] = act
    down = jnp.dot(act_ref[...], wdn_ref[...], preferred_element_type=F32)
    y = _rmsnorm(y_ref[...].reshape(m, D_MODEL) + down, nfin_ref[...])
    y_ref[...] = y.reshape(nb, tt, D_MODEL)


def _post(x, attn, u, ga, gb, state, weights, nb, tt, pos0):
    b, t, _ = x.shape
    (w_a, w_b, pool_w, pool_scale, w_o, norm_ffn, w_up, conv_w, conv_b, w_down, norm_final) = weights
    tile = lambda w: pl.BlockSpec((nb, tt, w), lambda bi, ti: (bi, ti, 0))
    per_seq = lambda r, w: pl.BlockSpec((nb, r, w), lambda bi, ti: (bi, 0, 0))
    consts = [w_a, w_b, pool_w, pool_scale, w_o, norm_ffn, w_up, conv_w, conv_b, w_down, norm_final]
    return pl.pallas_call(
        functools.partial(_post_kernel, pos0=pos0, has_state=state is not None),
        grid=(b // nb, t // tt),
        in_specs=[tile(D_MODEL), tile(ATTN_WIDTH), tile(POOL_WIDTH), tile(D_MODEL), tile(D_MODEL)]
                 + ([per_seq(POOL_STATE, POOL_WIDTH), per_seq(CONV_WIDTH - 1, 2 * D_FF)] if state else [])
                 + [_const_spec(c.shape) for c in consts],
        out_specs=[tile(D_MODEL), per_seq(POOL_STATE, POOL_WIDTH), per_seq(CONV_WIDTH - 1, 2 * D_FF)],
        out_shape=[jax.ShapeDtypeStruct((b, t, D_MODEL), F32),
                   jax.ShapeDtypeStruct((b, POOL_STATE, POOL_WIDTH), F32),
                   jax.ShapeDtypeStruct((b, CONV_WIDTH - 1, 2 * D_FF), F32)],
        scratch_shapes=[pltpu.VMEM((nb, POOL_HALO, POOL_WIDTH), F32),
                        pltpu.VMEM((nb, CONV_HALO, 2 * D_FF), F32),
                        pltpu.VMEM((nb * tt, D_MODEL), BF16),
                        pltpu.VMEM((nb * tt, D_FF), BF16)],
        compiler_params=pltpu.CompilerParams(dimension_semantics=("arbitrary", "arbitrary"),
                                             vmem_limit_bytes=VMEM_LIMIT),
        name="post",
    )(x, attn, u, ga, gb, *(state or ()), *consts)


def _stream(x, past, state, norm_mix, w_in, post_weights, *, tm, nb, tt, pos0):
    b, t, _ = x.shape
    m = b * t
    kv_transposed = past is None
    q, k, v, u, kb, vb, ga, gb = _inproj(x.reshape(m, D_MODEL), norm_mix, w_in, tm, t, kv_transposed)
    seq = lambda a: a.reshape(b, t, a.shape[-1])
    if past is None:
        side = [w for w in post_weights if isinstance(w, tuple)]
        attn, narrow = _attn_prompt(seq(q), seq(kb), seq(vb), blk=256, q_blocks=4, side=side)
        narrow = iter(narrow)
        post_weights = tuple(next(narrow) if isinstance(w, tuple) else w for w in post_weights)
        heads = lambda a: a.reshape(1, b, N_HEADS, HEAD_DIM, t).transpose(0, 1, 4, 2, 3)
    else:
        time_last = lambda c: c.transpose(0, 2, 3, 1).reshape(b, ATTN_WIDTH, c.shape[1])
        attn = _attn_sample(seq(q), seq(kb), seq(vb), time_last(past[0]), time_last(past[1]),
                            blk=256, new_blk=128)
        heads = lambda a: a.reshape(1, b, t, N_HEADS, HEAD_DIM)
    y, ptail, ctail = _post(x, attn, seq(u), seq(ga), seq(gb), state, post_weights, nb, tt, pos0)
    return (y, heads(k), heads(v), ptail[None], ctail[None]), post_weights


def kernel(x_prompt, x_sample, cache_k, cache_v, state_pool, state_conv, norm_mix, w_in, w_a, w_b,
           pool_w, pool_scale, w_o, norm_ffn, w_up, conv_w, conv_b, w_down, norm_final):
    assert w_in.shape[0] == 1, "single-layer trunk"
    row = lambda a: a.reshape(1, -1)
    post_weights = (w_a[0], w_b[0], pool_w[0], row(pool_scale[0]), w_o[0], row(norm_ffn[0]),
                    (w_up[0], 1.0), conv_w[0], row(conv_b[0]), (w_down[0], 0.5), row(norm_final))
    (yp, kp, vp, pp, cp), post_weights = _stream(x_prompt, None, None, norm_mix[0], w_in[0],
                                                 post_weights, tm=1024, nb=1, tt=512, pos0=0)
    bs, ts, _ = x_sample.shape
    (ys, ks, vs, ps, cs), _ = _stream(x_sample, (cache_k[0], cache_v[0]), (state_pool[0], state_conv[0]),
                                      norm_mix[0], w_in[0], post_weights,
                                      tm=bs * ts, nb=bs, tt=ts, pos0=cache_k.shape[2])
    return (yp, ys, kp, vp, pp, cp, ks, vs, ps, cs)
```

```python
import functools

import jax
import jax.numpy as jnp
from jax import lax
from jax.experimental import pallas as pl
from jax.experimental.pallas import tpu as pltpu

D_MODEL = 1024
N_HEADS = 8
HEAD_DIM = 64
ATTN_WIDTH = N_HEADS * HEAD_DIM
POOL_WINDOWS = (2, 4, 8, 16)
POOL_WIDTH = 512
POOL_GROUP = POOL_WIDTH // len(POOL_WINDOWS)
POOL_STATE = max(POOL_WINDOWS) - 1
D_FF = 2816
CONV_WIDTH = 3
EPS = 1e-6
IN_WIDTH = 3 * ATTN_WIDTH + POOL_WIDTH + 2 * D_MODEL

LANES = 128
BF16_SUBLANES = 16
POOL_HALO = 16
CONV_HALO = 8
FF_CHUNK = 256
VMEM_LIMIT = 56 * 1024 * 1024
LOG2E = 1.4426950408889634
DEAD_LOG2 = 160.0
SOFTPLUS_CLAMP = 126.0

F32 = jnp.float32
BF16 = jnp.bfloat16


def _rmsnorm(x, g):
    return x * lax.rsqrt(jnp.mean(x * x, axis=-1, keepdims=True) + EPS) * g


def _const_spec(shape):
    nd = len(shape)
    return pl.BlockSpec(shape, lambda *_: (0,) * nd, pipeline_mode=pl.Buffered(1))


def _inproj_kernel(x_ref, g_ref, w_ref, q_ref, k_ref, v_ref, u_ref, kb_ref, vb_ref, ga_ref, gb_ref,
                   *, kv_transposed):
    h = _rmsnorm(x_ref[...], g_ref[...]).astype(BF16)

    def proj(lo, hi):
        return jnp.dot(h, w_ref[:, lo:hi].astype(BF16), preferred_element_type=F32)

    a = ATTN_WIDTH
    u0 = 3 * a
    g0 = u0 + POOL_WIDTH
    ga_ref[...] = jax.nn.sigmoid(proj(g0, g0 + D_MODEL)).astype(BF16)
    gb_ref[...] = jax.nn.sigmoid(proj(g0 + D_MODEL, g0 + 2 * D_MODEL)).astype(BF16)
    k = proj(a, 2 * a)
    kb_ref[...] = k.astype(BF16)
    v = proj(2 * a, 3 * a)
    vb_ref[...] = v.astype(BF16)
    if kv_transposed:
        k_ref[0] = k.T
        v_ref[0] = v.T
    else:
        k_ref[...] = k
        v_ref[...] = v
    q_ref[...] = (proj(0, a) * (HEAD_DIM ** -0.5 * LOG2E)).astype(BF16)
    u_ref[...] = proj(u0, u0 + POOL_WIDTH)


def _inproj(x2d, norm_mix, w_in, tm, seq_len, kv_transposed):
    m = x2d.shape[0]
    row = lambda w: pl.BlockSpec((tm, w), lambda i: (i, 0))
    outs = [(ATTN_WIDTH, BF16), (ATTN_WIDTH, F32), (ATTN_WIDTH, F32), (POOL_WIDTH, F32),
            (ATTN_WIDTH, BF16), (ATTN_WIDTH, BF16), (D_MODEL, BF16), (D_MODEL, BF16)]
    out_specs = [row(w) for w, _ in outs]
    out_shape = [jax.ShapeDtypeStruct((m, w), dt) for w, dt in outs]
    if kv_transposed:
        tiles = seq_len // tm
        for i in (1, 2):
            out_specs[i] = pl.BlockSpec((1, ATTN_WIDTH, tm), lambda i: (i // tiles, 0, i % tiles))
            out_shape[i] = jax.ShapeDtypeStruct((m // seq_len, ATTN_WIDTH, seq_len), F32)
    return pl.pallas_call(
        functools.partial(_inproj_kernel, kv_transposed=kv_transposed),
        grid=(m // tm,),
        in_specs=[row(D_MODEL), _const_spec((1, D_MODEL)), _const_spec((D_MODEL, IN_WIDTH))],
        out_specs=out_specs,
        out_shape=out_shape,
        compiler_params=pltpu.CompilerParams(dimension_semantics=("arbitrary",),
                                             vmem_limit_bytes=VMEM_LIMIT),
        name="inproj",
    )(x2d, norm_mix.reshape(1, D_MODEL), w_in)


_NT = (((1,), (1,)), ((), ()))


def _softplus2(z):
    return jnp.maximum(jnp.log2(1.0 + jnp.exp2(jnp.minimum(z, SOFTPLUS_CLAMP))), z)


def _scores(qrows, kblk, k_transposed):
    if k_transposed:
        return jnp.dot(qrows, kblk, preferred_element_type=F32)
    return lax.dot_general(qrows, kblk, _NT, preferred_element_type=F32)


def _fail_bits(z, mask):
    sp = _softplus2(z)
    return sp if mask is None else jnp.where(mask, sp, 0.0)


def _upto(sp, tri):
    return jnp.dot(sp.astype(BF16), tri, preferred_element_type=F32)


def _weights(z, upto, carry, mask):
    a = jnp.exp2(z - upto - carry)
    if mask is not None:
        a = jnp.where(mask, a, 0.0)
    return a.astype(BF16)


def _apply(a, vblk, v_transposed):
    if v_transposed:
        return lax.dot_general(a, vblk, _NT, preferred_element_type=F32)
    return jnp.dot(a, vblk, preferred_element_type=F32)


def _stick_scores(qrows, kblk, tri, mask, k_transposed):
    z = _scores(qrows, kblk, k_transposed)
    sp = _fail_bits(z, mask)
    return z, sp, _upto(sp, tri)


def _stick_values(z, upto, carry, vblk, mask, v_transposed):
    return _apply(_weights(z, upto, carry, mask), vblk, v_transposed)


def _row_sum(sp):
    return jnp.sum(sp, axis=1, keepdims=True)


N_CHAIN_STAGES = 5


def _chain(q, kblk, vblk, tri, mask, carry_of, transposed):
    z = _scores(q, kblk, transposed)
    yield None
    sp = _fail_bits(z, mask)
    yield None
    upto = _upto(sp, tri)
    yield upto[:, :1]
    a = _weights(z, upto, carry_of(), mask)
    yield None
    yield _apply(a, vblk, transposed)


def _wavefront(chains):
    for step in range(len(chains) + N_CHAIN_STAGES - 1):
        for i, (stages, got) in enumerate(chains):
            if 0 <= step - i < N_CHAIN_STAGES:
                got.append(next(stages))


def _attn_prompt_kernel(q_ref, k_ref, v_ref, tri_ref, *rest, blk, side_scales):
    n_side = len(side_scales)
    wide_refs, o_ref, narrow_refs = rest[:n_side], rest[n_side], rest[n_side + 1:]
    for wide_ref, narrow_ref, scale in zip(wide_refs, narrow_refs, side_scales):
        narrow_ref[...] = (wide_ref[...] * scale).astype(BF16)

    n_q = q_ref.shape[1] // blk
    n_tiles = q_ref.shape[2] // LANES
    rows = 2 * blk
    tri = tri_ref[...]
    lower = lax.broadcasted_iota(jnp.int32, (blk, LANES), 1) < HEAD_DIM
    r = lax.broadcasted_iota(jnp.int32, (rows, blk), 0)
    c = lax.broadcasted_iota(jnp.int32, (rows, blk), 1)
    diag_mask = c < (r & (blk - 1))
    no_carry = jnp.zeros((rows, 1), F32)

    def q_rows(b, t):
        q2 = q_ref[0, b * blk:(b + 1) * blk, t * LANES:(t + 1) * LANES]
        zero = jnp.zeros_like(q2)
        return jnp.concatenate([jnp.where(lower, q2, zero), jnp.where(lower, zero, q2)], axis=0)

    def kv(j, t):
        start = pl.multiple_of(j * blk, blk)
        lanes = slice(t * LANES, (t + 1) * LANES)
        return k_ref[0, pl.ds(start, blk), lanes], v_ref[0, pl.ds(start, blk), lanes]

    def newest_two():
        qis = [pl.program_id(1) * n_q + b for b in range(n_q)]
        qs, chains = [], []
        for b in range(n_q):
            for t in range(n_tiles):
                q = q_rows(b, t)
                kd, vd = kv(qis[b], t)
                kp, vp = kv(jnp.maximum(qis[b] - 1, 0), t)
                diag = (_chain(q, kd, vd, tri, diag_mask, lambda: no_carry, False), [])
                prev = (_chain(q, kp, vp, tri, None, lambda got=diag[1]: got[2], False), [])
                qs.append(q)
                chains += [diag, prev]
        _wavefront(chains)
        started = []
        for b in range(n_q):
            has_prev = qis[b] > 0
            accs, carries = [], []
            for t in range(n_tiles):
                diag, prev = (got for _, got in chains[2 * (b * n_tiles + t):][:2])
                accs.append(diag[-1] + jnp.where(has_prev, prev[-1], 0.0))
                carries.append(diag[2] + jnp.where(has_prev, prev[2], 0.0))
            started.append((qis[b], qs[b * n_tiles:(b + 1) * n_tiles], accs, carries))
        return started

    def live(carries):
        return jnp.min(functools.reduce(jnp.minimum, carries)) < DEAD_LOG2

    def older(qi, qs, accs, carries):
        def cond(state):
            j, _, _, alive = state
            return jnp.logical_and(j >= 0, alive)

        def body(state):
            j, accs, carries, _ = state
            new_accs, new_carries = [], []
            for t in range(n_tiles):
                kblk, vblk = kv(j, t)
                z, sp, upto = _stick_scores(qs[t], kblk, tri, None, False)
                new_accs.append(accs[t] + _stick_values(z, upto, carries[t], vblk, None, False))
                new_carries.append(carries[t] + _row_sum(sp))
            return j - 1, tuple(new_accs), tuple(new_carries), live(new_carries)

        return lax.while_loop(cond, body, (qi - 2, tuple(accs), tuple(carries), True))[1]

    def write(b, accs):
        for t in range(n_tiles):
            o_ref[0, b * blk:(b + 1) * blk, t * LANES:(t + 1) * LANES] = jnp.where(
                lower, accs[t][:blk], accs[t][blk:]).astype(o_ref.dtype)

    for b, (qi, qs, accs, carries) in enumerate(newest_two()):
        write(b, accs)

        @pl.when(jnp.logical_and(qi >= 2, live(carries)))
        def _():
            write(b, older(qi, qs, accs, carries))


def _tri(blk):
    j = lax.broadcasted_iota(jnp.int32, (blk, blk), 0)
    s = lax.broadcasted_iota(jnp.int32, (blk, blk), 1)
    return (j >= s).astype(BF16)


def _slab_rows(n_rows, n_steps):
    return next(r for r in range(BF16_SUBLANES, n_rows + 1, BF16_SUBLANES)
                if n_rows % r == 0 and n_rows // r <= n_steps)


def _attn_prompt(q, kb, vb, blk, q_blocks, side):
    b, t, _ = q.shape
    steps_q = t // (q_blocks * blk)
    qspec = pl.BlockSpec((1, q_blocks * blk, ATTN_WIDTH), lambda bi, qi: (bi, qi, 0))
    kvspec = pl.BlockSpec((1, t, ATTN_WIDTH), lambda bi, qi: (bi, 0, 0))

    def slab(w):
        rows = _slab_rows(w.shape[0], b * steps_q)
        last = w.shape[0] // rows - 1
        return pl.BlockSpec((rows, w.shape[1]), lambda bi, qi: (jnp.minimum(bi * steps_q + qi, last), 0))

    slabs = [slab(w) for w, _ in side]
    attn, *narrow = pl.pallas_call(
        functools.partial(_attn_prompt_kernel, blk=blk, side_scales=tuple(scale for _, scale in side)),
        grid=(b, steps_q),
        in_specs=[qspec, kvspec, kvspec, _const_spec((blk, blk))] + slabs,
        out_specs=[qspec] + slabs,
        out_shape=[jax.ShapeDtypeStruct((b, t, ATTN_WIDTH), BF16)]
                  + [jax.ShapeDtypeStruct(w.shape, BF16) for w, _ in side],
        compiler_params=pltpu.CompilerParams(
            dimension_semantics=("arbitrary", "arbitrary"),
            vmem_limit_bytes=VMEM_LIMIT),
        name="attn_prompt",
    )(q, kb, vb, _tri(blk), *[w for w, _ in side])
    return attn, narrow


def _attn_sample_kernel(q_ref, kn_ref, vn_ref, ck_ref, cv_ref, tri_ref, o_ref, carry_ref, *, blk, new_blk):
    n_seq, tq, _ = q_ref.shape
    rows = N_HEADS * tq
    r = lax.broadcasted_iota(jnp.int32, (rows, ATTN_WIDTH), 0)
    c = lax.broadcasted_iota(jnp.int32, (rows, ATTN_WIDTH), 1)
    tq_bits = tq.bit_length() - 1
    head_bits = HEAD_DIM.bit_length() - 1
    assert tq == 1 << tq_bits
    head_sel = (r >> tq_bits) == (c >> head_bits)
    tri = tri_ref[...]
    tri_new = tri[:new_blk, :new_blk]
    rn = lax.broadcasted_iota(jnp.int32, (rows, new_blk), 0)
    cn = lax.broadcasted_iota(jnp.int32, (rows, new_blk), 1)
    new_mask = cn < (rn & (tq - 1))
    pad = jnp.zeros((new_blk - tq, ATTN_WIDTH), BF16)
    no_carry = jnp.zeros((rows, 1), F32)
    co = lax.broadcasted_iota(jnp.int32, (tq, ATTN_WIDTH), 1) >> head_bits
    window = ck_ref.shape[2]

    def carry_after(earlier):
        return lambda: functools.reduce(jnp.add, [got[2] for got in earlier], no_carry)

    per_seq, chains = [], []
    for s in range(n_seq):
        q = q_ref[s]
        qrep = jnp.concatenate([q] * N_HEADS, axis=0)
        qrows = jnp.where(head_sel, qrep, jnp.zeros_like(qrep))
        mine = [(_chain(qrows, jnp.concatenate([kn_ref[s], pad], axis=0),
                        jnp.concatenate([vn_ref[s], pad], axis=0), tri_new, new_mask,
                        lambda: no_carry, False), [])]
        for j in reversed(range(window // blk)):
            kblk = ck_ref[s, :, j * blk:(j + 1) * blk].astype(BF16)
            vblk = cv_ref[s, :, j * blk:(j + 1) * blk].astype(BF16)
            mine.append((_chain(qrows, kblk, vblk, tri, None,
                                carry_after([got for _, got in mine]), True), []))
        per_seq.append(mine)
        chains += mine
    _wavefront(chains)

    for s, mine in enumerate(per_seq):
        parts = [got[-1] for _, got in mine]
        while len(parts) > 1:
            parts = [functools.reduce(jnp.add, parts[i:i + 2]) for i in range(0, len(parts), 2)]
        acc = parts[0]
        out = jnp.zeros((tq, ATTN_WIDTH), F32)
        for h in range(N_HEADS):
            out = jnp.where(co == h, acc[h * tq:(h + 1) * tq], out)
        o_ref[s] = out.astype(o_ref.dtype)
        carry = carry_after([got for _, got in mine])()
        carry_ref[s] = jnp.broadcast_to(jnp.min(carry, axis=0, keepdims=True), carry_ref.shape[1:])


def _attn_sample_window(q, kb, vb, cache_k, cache_v, blk, new_blk, window, n_seq):
    b, tq, _ = q.shape
    past = cache_k.shape[2]
    new = pl.BlockSpec((n_seq, tq, ATTN_WIDTH), lambda bi: (bi, 0, 0))
    old = pl.BlockSpec((n_seq, ATTN_WIDTH, window), lambda bi: (bi, 0, past // window - 1))
    low = pl.BlockSpec((n_seq, 8, LANES), lambda bi: (bi, 0, 0))
    return pl.pallas_call(
        functools.partial(_attn_sample_kernel, blk=blk, new_blk=new_blk),
        grid=(b // n_seq,),
        in_specs=[new, new, new, old, old, _const_spec((blk, blk))],
        out_specs=[new, low],
        out_shape=[jax.ShapeDtypeStruct((b, tq, ATTN_WIDTH), BF16),
                   jax.ShapeDtypeStruct((b, 8, LANES), F32)],
        compiler_params=pltpu.CompilerParams(dimension_semantics=("arbitrary",),
                                             vmem_limit_bytes=VMEM_LIMIT),
        name="attn_sample",
    )(q, kb, vb, cache_k, cache_v, _tri(blk))


def _attn_sample(q, kb, vb, cache_k, cache_v, blk, new_blk):
    past = cache_k.shape[2]
    recent, min_carry = _attn_sample_window(q, kb, vb, cache_k, cache_v, blk, new_blk, blk, 4)
    if past == blk:
        return recent
    return lax.cond(jnp.min(min_carry) < DEAD_LOG2,
                    lambda: _attn_sample_window(q, kb, vb, cache_k, cache_v, blk, new_blk, past, 2)[0],
                    lambda: recent)


def _with_history(hist, cur):
    nb, h, c = hist.shape
    t = cur.shape[1]
    return jnp.concatenate([hist, cur], axis=1).reshape(nb * (h + t), c)


def _drop_history(flat, nb, h, t):
    c = flat.shape[-1]
    return flat.reshape(nb, h + t, c)[:, h:, :].reshape(nb * t, c)


def _twice_gelu_tanh(x):
    c = 0.7978845608028654
    return x * (1.0 + jnp.tanh(x * (c + (c * 0.044715) * (x * x))))


def _post_kernel(x_ref, attn_ref, u_ref, ga_ref, gb_ref, *rest, pos0, has_state):
    state_refs, rest = (rest[:2], rest[2:]) if has_state else ((), rest)
    (wa_ref, wb_ref, pw_ref, ps_ref, wo_ref, nffn_ref, wup_ref, cw_ref, cb_ref, wdn_ref, nfin_ref,
     y_ref, ptail_ref, ctail_ref, uhist_ref, chist_ref, h2_ref, act_ref) = rest
    nb, tt, _ = x_ref.shape
    m = nb * tt
    ti = pl.program_id(1)

    @pl.when(ti == 0)
    def _():
        uhist_ref[...] = jnp.zeros(uhist_ref.shape, F32)
        chist_ref[...] = jnp.zeros(chist_ref.shape, F32)
        if has_state:
            uhist_ref[:, POOL_HALO - POOL_STATE:, :] = state_refs[0][...]
            chist_ref[:, CONV_HALO - (CONV_WIDTH - 1):, :] = state_refs[1][...]

    branch_a = jnp.dot(attn_ref[...].reshape(m, ATTN_WIDTH), wa_ref[...].astype(BF16), preferred_element_type=F32)
    x = x_ref[...].reshape(m, D_MODEL)
    u3 = u_ref[...]
    u = u3.reshape(m, POOL_WIDTH)

    ext = _with_history(uhist_ref[...], u3)
    row = lax.broadcasted_iota(jnp.int32, (m, POOL_GROUP), 0)
    pos = pos0 + ti * tt + (row & (tt - 1))
    pmix = []
    for g, w in enumerate(POOL_WINDOWS):
        sl = slice(g * POOL_GROUP, (g + 1) * POOL_GROUP)
        s = ext[:, sl]
        shift = 1
        while shift < w:
            s = s + pltpu.roll(s, shift, axis=0)
            shift *= 2
        win = _drop_history(s, nb, POOL_HALO, tt)
        cnt = jnp.minimum(pos + 1, w).astype(F32)
        pooled = win / cnt - u[:, sl]
        pmix.append(jnp.dot(pooled.astype(BF16), pw_ref[g].astype(BF16), preferred_element_type=F32))
    pmix = jnp.concatenate(pmix, axis=1) * ps_ref[...]
    branch_b = jnp.dot(pmix.astype(BF16), wb_ref[...].astype(BF16), preferred_element_type=F32)
    merged = (ga_ref[...].reshape(m, D_MODEL).astype(F32) * branch_a
              + gb_ref[...].reshape(m, D_MODEL).astype(F32) * branch_b)
    x1 = x + jnp.dot(merged.astype(BF16), wo_ref[...].astype(BF16), preferred_element_type=F32)

    tail = u3[:, tt - POOL_HALO:, :]
    uhist_ref[...] = tail
    ptail_ref[...] = tail[:, POOL_HALO - POOL_STATE:, :]

    y_ref[...] = x1.reshape(nb, tt, D_MODEL)
    h2_ref[...] = _rmsnorm(x1, nffn_ref[...]).astype(BF16)
    n_chunks = D_FF // FF_CHUNK

    def up_proj(j):
        cols = [slice(half * D_FF + j * FF_CHUNK, half * D_FF + (j + 1) * FF_CHUNK) for half in range(2)]
        return [(cs, jnp.dot(h2_ref[...], wup_ref[:, cs], preferred_element_type=F32)) for cs in cols]

    ups = up_proj(0)
    for j in range(n_chunks):
        cur, ups = ups, (up_proj(j + 1) if j + 1 < n_chunks else None)
        halves = []
        for cs, up in cur:
            up3 = up.reshape(nb, tt, FF_CHUNK)
            e = _with_history(chist_ref[:, :, cs], up3)
            conv = cb_ref[:, cs] + up * cw_ref[2:3, cs]
            for d in (1, 2):
                shifted = _drop_history(pltpu.roll(e, d, axis=0), nb, CONV_HALO, tt)
                conv = conv + shifted * cw_ref[2 - d:3 - d, cs]
            halves.append(conv)
            ctile = up3[:, tt - CONV_HALO:, :]
            chist_ref[:, :, cs] = ctile
            ctail_ref[:, :, cs] = ctile[:, CONV_HALO - (CONV_WIDTH - 1):, :]
        act = (_twice_gelu_tanh(halves[0]) * halves[1]).astype(BF16)
        act_ref[:, j * FF_CHUNK:(j + 1) * FF_CHUNK] = act
    down = jnp.dot(act_ref[...], wdn_ref[...], preferred_element_type=F32)
    y = _rmsnorm(y_ref[...].reshape(m, D_MODEL) + down, nfin_ref[...])
    y_ref[...] = y.reshape(nb, tt, D_MODEL)


def _post(x, attn, u, ga, gb, state, weights, nb, tt, pos0):
    b, t, _ = x.shape
    (w_a, w_b, pool_w, pool_scale, w_o, norm_ffn, w_up, conv_w, conv_b, w_down, norm_final) = weights
    tile = lambda w: pl.BlockSpec((nb, tt, w), lambda bi, ti: (bi, ti, 0))
    per_seq = lambda r, w: pl.BlockSpec((nb, r, w), lambda bi, ti: (bi, 0, 0))
    consts = [w_a, w_b, pool_w, pool_scale, w_o, norm_ffn, w_up, conv_w, conv_b, w_down, norm_final]
    return pl.pallas_call(
        functools.partial(_post_kernel, pos0=pos0, has_state=state is not None),
        grid=(b // nb, t // tt),
        in_specs=[tile(D_MODEL), tile(ATTN_WIDTH), tile(POOL_WIDTH), tile(D_MODEL), tile(D_MODEL)]
                 + ([per_seq(POOL_STATE, POOL_WIDTH), per_seq(CONV_WIDTH - 1, 2 * D_FF)] if state else [])
                 + [_const_spec(c.shape) for c in consts],
        out_specs=[tile(D_MODEL), per_seq(POOL_STATE, POOL_WIDTH), per_seq(CONV_WIDTH - 1, 2 * D_FF)],
        out_shape=[jax.ShapeDtypeStruct((b, t, D_MODEL), F32),
                   jax.ShapeDtypeStruct((b, POOL_STATE, POOL_WIDTH), F32),
                   jax.ShapeDtypeStruct((b, CONV_WIDTH - 1, 2 * D_FF), F32)],
        scratch_shapes=[pltpu.VMEM((nb, POOL_HALO, POOL_WIDTH), F32),
                        pltpu.VMEM((nb, CONV_HALO, 2 * D_FF), F32),
                        pltpu.VMEM((nb * tt, D_MODEL), BF16),
                        pltpu.VMEM((nb * tt, D_FF), BF16)],
        compiler_params=pltpu.CompilerParams(dimension_semantics=("arbitrary", "arbitrary"),
                                             vmem_limit_bytes=VMEM_LIMIT),
        name="post",
    )(x, attn, u, ga, gb, *(state or ()), *consts)


def _stream(x, past, state, norm_mix, w_in, post_weights, *, tm, nb, tt, pos0):
    b, t, _ = x.shape
    m = b * t
    kv_transposed = past is None
    q, k, v, u, kb, vb, ga, gb = _inproj(x.reshape(m, D_MODEL), norm_mix, w_in, tm, t, kv_transposed)
    seq = lambda a: a.reshape(b, t, a.shape[-1])
    if past is None:
        side = [w for w in post_weights if isinstance(w, tuple)]
        attn, narrow = _attn_prompt(seq(q), seq(kb), seq(vb), blk=256, q_blocks=4, side=side)
        narrow = iter(narrow)
        post_weights = tuple(next(narrow) if isinstance(w, tuple) else w for w in post_weights)
        heads = lambda a: a.reshape(1, b, N_HEADS, HEAD_DIM, t).transpose(0, 1, 4, 2, 3)
    else:
        time_last = lambda c: c.transpose(0, 2, 3, 1).reshape(b, ATTN_WIDTH, c.shape[1])
        attn = _attn_sample(seq(q), seq(kb), seq(vb), time_last(past[0]), time_last(past[1]),
                            blk=256, new_blk=128)
        heads = lambda a: a.reshape(1, b, t, N_HEADS, HEAD_DIM)
    y, ptail, ctail = _post(x, attn, seq(u), seq(ga), seq(gb), state, post_weights, nb, tt, pos0)
    return (y, heads(k), heads(v), ptail[None], ctail[None]), post_weights


def kernel(x_prompt, x_sample, cache_k, cache_v, state_pool, state_conv, norm_mix, w_in, w_a, w_b,
           pool_w, pool_scale, w_o, norm_ffn, w_up, conv_w, conv_b, w_down, norm_final):
    assert w_in.shape[0] == 1, "single-layer trunk"
    row = lambda a: a.reshape(1, -1)
    post_weights = (w_a[0], w_b[0], pool_w[0], row(pool_scale[0]), w_o[0], row(norm_ffn[0]),
                    (w_up[0], 1.0), conv_w[0], row(conv_b[0]), (w_down[0], 0.5), row(norm_final))
    (yp, kp, vp, pp, cp), post_weights = _stream(x_prompt, None, None, norm_mix[0], w_in[0],
                                                 post_weights, tm=1024, nb=1, tt=512, pos0=0)
    bs, ts, _ = x_sample.shape
    (ys, ks, vs, ps, cs), _ = _stream(x_sample, (cache_k[0], cache_v[0]), (state_pool[0], state_conv[0]),
                                      norm_mix[0], w_in[0], post_weights,
                                      tm=bs * ts, nb=bs, tt=ts, pos0=cache_k.shape[2])
    return (yp, ys, kp, vp, pp, cp, ks, vs, ps, cs)
```
